```python
import jax
import jax.numpy as jnp
from jax import lax
import numpy as np

D_MODEL = 1024
BATCH = 8
SEQ = 4096
DEPTH = 2

N_MIXERS = 2
HEAD_DIM = 64
MIX_HEADS = 12
MEM_HEADS = 4
MIX_WIDTH = MIX_HEADS * HEAD_DIM
MEM_WIDTH = MEM_HEADS * HEAD_DIM
MEM_TOKENS = 256
MLA_Q_RANK = 384
MLA_KV_RANK = 256
MLA_NOPE_DIM = 64
MLA_ROPE_DIM = 32
MLA_V_DIM = 64
MLA_QK_DIM = MLA_NOPE_DIM + MLA_ROPE_DIM
ROPE_BASE = 10000.0
Q_BLOCK = 128
RWKV_DECAY_RANK = 64
RWKV_A_RANK = 64
RWKV_GATE_RANK = 160
RWKV_GN_EPS = 64e-5
D_FF = 4 * D_MODEL
LN_EPS = 1e-5
RMS_EPS = 1e-6
ALPHA = (2.0 * DEPTH) ** 0.25
BETA = (8.0 * DEPTH) ** -0.25
N_MLA_LAYERS = (DEPTH + 1) // 2
N_RWKV_LAYERS = DEPTH // 2
MLA_IN_COLS = MLA_Q_RANK + MLA_KV_RANK + MLA_ROPE_DIM + MEM_WIDTH
RWKV_TM_COLS = 3 * MIX_WIDTH + RWKV_DECAY_RANK + RWKV_A_RANK + RWKV_GATE_RANK
RWKV_IN_COLS = RWKV_TM_COLS + MEM_WIDTH

kernel_name = "mla_rwkv7_interleaved_deepnorm_memory"


def _split(t, sizes):
    out, start = [], 0
    for s in sizes:
        out.append(t[..., start:start + s])
        start += s
    return out


def _layer_norm(t, g, b):
    tf = t.astype(jnp.float32)
    mu = jnp.mean(tf, axis=-1, keepdims=True)
    var = jnp.mean(jnp.square(tf - mu), axis=-1, keepdims=True)
    y = (tf - mu) * lax.rsqrt(var + LN_EPS)
    return (y * g.astype(jnp.float32) + b.astype(jnp.float32)).astype(t.dtype)


def _rms_norm(t, g):
    tf = t.astype(jnp.float32)
    y = tf * lax.rsqrt(jnp.mean(jnp.square(tf), axis=-1, keepdims=True) + RMS_EPS)
    return (y * g.astype(jnp.float32)).astype(t.dtype)


def _apply_rope(t, cos, sin):
    half = t.shape[-1] // 2
    t1, t2 = t[..., :half], t[..., half:]
    cos = cos.astype(t.dtype)
    sin = sin.astype(t.dtype)
    return jnp.concatenate([t1 * cos - t2 * sin, t1 * sin + t2 * cos], axis=-1)


def _token_shift(t):
    return jnp.pad(t, ((0, 0), (1, 0), (0, 0)))[:, :-1]


def _memory_attention(q, mem_k, mem_v):
    s = jnp.einsum("bthd,bmhd->bhtm", q, mem_k).astype(jnp.float32) * (HEAD_DIM ** -0.5)
    p = jax.nn.softmax(s, axis=-1).astype(mem_v.dtype)
    return jnp.einsum("bhtm,bmhd->bthd", p, mem_v)


def _causal_mla_attention(q_nope, q_rope, k_nope, k_rope, v):
    B, T, H, _ = q_nope.shape
    n_blocks = T // Q_BLOCK
    scale = MLA_QK_DIM ** -0.5
    key_pos = jnp.arange(T)

    def to_blocks(t):
        return jnp.moveaxis(t.reshape((B, n_blocks, Q_BLOCK) + t.shape[2:]), 1, 0)

    def one_block(args):
        qn, qr, start = args
        s = (jnp.einsum("bqhd,bkhd->bhqk", qn, k_nope)
             + jnp.einsum("bqhd,bkd->bhqk", qr, k_rope)).astype(jnp.float32) * scale
        q_pos = start + jnp.arange(Q_BLOCK)
        s = jnp.where(key_pos[None, :] <= q_pos[:, None], s, -1e30)
        p = jax.nn.softmax(s, axis=-1).astype(v.dtype)
        return jnp.einsum("bhqk,bkhd->bqhd", p, v)

    out = lax.map(one_block, (to_blocks(q_nope), to_blocks(q_rope), jnp.arange(n_blocks) * Q_BLOCK))
    return jnp.moveaxis(out, 0, 1).reshape(B, T, H, v.shape[-1])


def _rwkv7_scan(r, w, k, v, a, b):
    B, T, H, N = r.shape

    def step(S, inp):
        r_t, w_t, k_t, v_t, a_t, b_t = inp
        sa = jnp.einsum("bhij,bhj->bhi", S, a_t)
        S = S * w_t[:, :, None, :] + sa[..., None] * b_t[:, :, None, :] + v_t[..., None] * k_t[:, :, None, :]
        y = jnp.einsum("bhij,bhj->bhi", S, r_t)
        return S, y

    xs = tuple(jnp.moveaxis(t, 1, 0) for t in (r, w, k, v, a, b))
    S0 = jnp.zeros((B, H, N, N), jnp.float32)
    _, ys = lax.scan(step, S0, xs)
    return jnp.moveaxis(ys, 0, 1)


def _mla_mixer(h, cos, sin, mem_k, mem_v, w_in, q_norm, w_q_up, kv_norm, w_kv_up, w_out):
    B, T, _ = h.shape
    c_q, c_kv, k_rope, q_mem = _split(h @ w_in, (MLA_Q_RANK, MLA_KV_RANK, MLA_ROPE_DIM, MEM_WIDTH))
    q = (_rms_norm(c_q, q_norm) @ w_q_up).reshape(B, T, MIX_HEADS, MLA_QK_DIM)
    q_nope, q_rope = q[..., :MLA_NOPE_DIM], q[..., MLA_NOPE_DIM:]
    kv = (_rms_norm(c_kv, kv_norm) @ w_kv_up).reshape(B, T, MIX_HEADS, MLA_NOPE_DIM + MLA_V_DIM)
    k_nope, v = kv[..., :MLA_NOPE_DIM], kv[..., MLA_NOPE_DIM:]
    q_rope = _apply_rope(q_rope, cos[:, :, None, :], sin[:, :, None, :])
    k_rope = _apply_rope(k_rope, cos, sin)
    o_mix = _causal_mla_attention(q_nope, q_rope, k_nope, k_rope, v)
    o_mem = _memory_attention(q_mem.reshape(B, T, MEM_HEADS, HEAD_DIM), mem_k, mem_v)
    o = jnp.concatenate([o_mix.reshape(B, T, MIX_WIDTH), o_mem.reshape(B, T, MEM_WIDTH)], axis=-1)
    return o @ w_out


def _rwkv7_mixer(h, mem_k, mem_v, w_in, mu, w0, w2, a0, a2, g2, k_k, k_a, r_k, gn_g, gn_b, w_out):
    B, T, _ = h.shape
    f32 = jnp.float32
    proj = h @ w_in
    p_tm, q_mem = proj[..., :RWKV_TM_COLS], proj[..., RWKV_TM_COLS:]
    p_tm = p_tm + (_token_shift(p_tm) - p_tm) * mu
    r, k, v, hw, ha, hg = _split(p_tm, (MIX_WIDTH, MIX_WIDTH, MIX_WIDTH, RWKV_DECAY_RANK, RWKV_A_RANK, RWKV_GATE_RANK))
    w_log = -jax.nn.softplus(-(w0 + jnp.tanh(hw) @ w2).astype(f32)) - 0.5
    decay = jnp.exp(-jnp.exp(w_log))
    a = jax.nn.sigmoid((a0 + ha @ a2).astype(f32))
    g = jax.nn.sigmoid(hg) @ g2

    def heads(t):
        return t.reshape(B, T, MIX_HEADS, HEAD_DIM)

    k = k.astype(f32)
    kk = heads(k * k_k.astype(f32))
    kk = kk / jnp.maximum(jnp.sqrt(jnp.sum(jnp.square(kk), axis=-1, keepdims=True)), 1e-12)
    k = heads(k * (1.0 + (a - 1.0) * k_a.astype(f32)))
    r = heads(r.astype(f32))
    v = heads(v.astype(f32))
    a = heads(a)
    y = _rwkv7_scan(r, heads(decay), k, v, -kk, kk * a)
    mean = jnp.mean(y, axis=-1, keepdims=True)
    var = jnp.mean(jnp.square(y - mean), axis=-1, keepdims=True)
    y = (y - mean) * lax.rsqrt(var + RWKV_GN_EPS)
    y = y * gn_g.astype(f32).reshape(MIX_HEADS, HEAD_DIM) + gn_b.astype(f32).reshape(MIX_HEADS, HEAD_DIM)
    y = y + jnp.sum(r * k * r_k.astype(f32), axis=-1, keepdims=True) * v
    y = y.reshape(B, T, MIX_WIDTH).astype(h.dtype) * g
    o_mem = _memory_attention(q_mem.reshape(B, T, MEM_HEADS, HEAD_DIM), mem_k, mem_v)
    o = jnp.concatenate([y, o_mem.reshape(B, T, MEM_WIDTH)], axis=-1)
    return o @ w_out


def setup_inputs(seed: int = 0) -> dict:
    key = jax.random.key(seed)
    ks = iter(jax.random.split(key, 48))

    def nrm(shape, scale):
        return scale * jax.random.normal(next(ks), shape, jnp.float32)

    NA, NB, D = N_MLA_LAYERS, N_RWKV_LAYERS, D_MODEL
    x = nrm((BATCH, SEQ, D), 1.0)
    mem = nrm((BATCH, MEM_TOKENS, D), 1.0)
    positions = jnp.broadcast_to(jnp.arange(SEQ, dtype=jnp.int32)[None, :], (BATCH, SEQ))
    mem_ln_g = 1.0 + nrm((D,), 0.02)
    mem_ln_b = nrm((D,), 0.02)
    w_mem_kv = jnp.concatenate([nrm((D, MEM_WIDTH), D ** -0.5), nrm((D, MEM_WIDTH), BETA * D ** -0.5)], axis=-1)
    mla_w_in = nrm((NA, D, MLA_IN_COLS), D ** -0.5)
    mla_q_norm = 1.0 + nrm((NA, MLA_Q_RANK), 0.02)
    mla_w_q_up = nrm((NA, MLA_Q_RANK, MIX_HEADS * MLA_QK_DIM), MLA_Q_RANK ** -0.5)
    mla_kv_norm = 1.0 + nrm((NA, MLA_KV_RANK), 0.02)
    kv_k = nrm((NA, MLA_KV_RANK, MIX_HEADS, MLA_NOPE_DIM), MLA_KV_RANK ** -0.5)
    kv_v = nrm((NA, MLA_KV_RANK, MIX_HEADS, MLA_V_DIM), BETA * MLA_KV_RANK ** -0.5)
    mla_w_kv_up = jnp.concatenate([kv_k, kv_v], axis=-1).reshape(NA, MLA_KV_RANK, MIX_HEADS * (MLA_NOPE_DIM + MLA_V_DIM))
    rwkv_w_in = jnp.concatenate([
        nrm((NB, D, 2 * MIX_WIDTH), D ** -0.5),
        nrm((NB, D, MIX_WIDTH), BETA * D ** -0.5),
        nrm((NB, D, RWKV_DECAY_RANK + RWKV_A_RANK + RWKV_GATE_RANK + MEM_WIDTH), D ** -0.5),
    ], axis=-1)
    rwkv_mu = jax.random.uniform(next(ks), (NB, RWKV_TM_COLS), jnp.float32)
    rwkv_w0 = jnp.broadcast_to(jnp.linspace(-6.0, -1.0, MIX_WIDTH, dtype=jnp.float32), (NB, MIX_WIDTH)) + nrm((NB, MIX_WIDTH), 0.1)
    rwkv_w2 = nrm((NB, RWKV_DECAY_RANK, MIX_WIDTH), 0.5 * RWKV_DECAY_RANK ** -0.5)
    rwkv_a0 = nrm((NB, MIX_WIDTH), 0.1)
    rwkv_a2 = nrm((NB, RWKV_A_RANK, MIX_WIDTH), RWKV_A_RANK ** -0.5)
    rwkv_g2 = nrm((NB, RWKV_GATE_RANK, MIX_WIDTH), RWKV_GATE_RANK ** -0.5)
    rwkv_k_k = 0.85 + nrm((NB, MIX_WIDTH), 0.05)
    rwkv_k_a = 1.0 + nrm((NB, MIX_WIDTH), 0.05)
    rwkv_r_k = nrm((NB, MIX_HEADS, HEAD_DIM), 0.1)
    rwkv_gn_g = 1.0 + nrm((NB, MIX_WIDTH), 0.02)
    rwkv_gn_b = nrm((NB, MIX_WIDTH), 0.02)
    w_out = nrm((DEPTH, MIX_WIDTH + MEM_WIDTH, D), BETA * (MIX_WIDTH + MEM_WIDTH) ** -0.5)
    ln1_g = 1.0 + nrm((DEPTH, D), 0.02)
    ln1_b = nrm((DEPTH, D), 0.02)
    w_ff1 = nrm((DEPTH, D, D_FF), D ** -0.5)
    w_ff2 = nrm((DEPTH, D_FF, D), BETA * D_FF ** -0.5)
    ln2_g = 1.0 + nrm((DEPTH, D), 0.02)
    ln2_b = nrm((DEPTH, D), 0.02)
    return {
        "x": x, "mem": mem, "positions": positions,
        "mem_ln_g": mem_ln_g, "mem_ln_b": mem_ln_b, "w_mem_kv": w_mem_kv,
        "mla_w_in": mla_w_in, "mla_q_norm": mla_q_norm, "mla_w_q_up": mla_w_q_up,
        "mla_kv_norm": mla_kv_norm, "mla_w_kv_up": mla_w_kv_up,
        "rwkv_w_in": rwkv_w_in, "rwkv_mu": rwkv_mu, "rwkv_w0": rwkv_w0, "rwkv_w2": rwkv_w2,
        "rwkv_a0": rwkv_a0, "rwkv_a2": rwkv_a2, "rwkv_g2": rwkv_g2, "rwkv_k_k": rwkv_k_k,
        "rwkv_k_a": rwkv_k_a, "rwkv_r_k": rwkv_r_k, "rwkv_gn_g": rwkv_gn_g, "rwkv_gn_b": rwkv_gn_b,
        "w_out": w_out, "ln1_g": ln1_g, "ln1_b": ln1_b,
        "w_ff1": w_ff1, "w_ff2": w_ff2, "ln2_g": ln2_g, "ln2_b": ln2_b,
    }


def reference(x, mem, positions, mem_ln_g, mem_ln_b, w_mem_kv,
              mla_w_in, mla_q_norm, mla_w_q_up, mla_kv_norm, mla_w_kv_up,
              rwkv_w_in, rwkv_mu, rwkv_w0, rwkv_w2, rwkv_a0, rwkv_a2, rwkv_g2,
              rwkv_k_k, rwkv_k_a, rwkv_r_k, rwkv_gn_g, rwkv_gn_b,
              w_out, ln1_g, ln1_b, w_ff1, w_ff2, ln2_g, ln2_b):
    B, M = mem.shape[0], mem.shape[1]
    mem_kv = _layer_norm(mem, mem_ln_g, mem_ln_b) @ w_mem_kv
    mem_k = mem_kv[..., :MEM_WIDTH].reshape(B, M, MEM_HEADS, HEAD_DIM)
    mem_v = mem_kv[..., MEM_WIDTH:].reshape(B, M, MEM_HEADS, HEAD_DIM)
    half = MLA_ROPE_DIM // 2
    inv_freq = ROPE_BASE ** (-jnp.arange(half, dtype=jnp.float32) * 2.0 / MLA_ROPE_DIM)
    ang = positions.astype(jnp.float32)[..., None] * inv_freq
    cos, sin = jnp.cos(ang), jnp.sin(ang)

    for i in range(DEPTH):
        j = i // N_MIXERS
        if i % N_MIXERS == 0:
            mix = _mla_mixer(x, cos, sin, mem_k, mem_v, mla_w_in[j], mla_q_norm[j], mla_w_q_up[j],
                             mla_kv_norm[j], mla_w_kv_up[j], w_out[i])
        else:
            mix = _rwkv7_mixer(x, mem_k, mem_v, rwkv_w_in[j], rwkv_mu[j], rwkv_w0[j], rwkv_w2[j],
                               rwkv_a0[j], rwkv_a2[j], rwkv_g2[j], rwkv_k_k[j], rwkv_k_a[j],
                               rwkv_r_k[j], rwkv_gn_g[j], rwkv_gn_b[j], w_out[i])
        x = _layer_norm(ALPHA * x + mix, ln1_g[i], ln1_b[i])
        ff = jnp.square(jax.nn.relu(x @ w_ff1[i])) @ w_ff2[i]
        x = _layer_norm(ALPHA * x + ff, ln2_g[i], ln2_b[i])
    return x
```

```python
import functools

import jax
import jax.numpy as jnp
from jax import lax
from jax.experimental import pallas as pl
from jax.experimental.pallas import tpu as pltpu

F32 = jnp.float32
BF16 = jnp.bfloat16

HEAD_DIM = 64
MIX_HEADS = 12
MEM_HEADS = 4
MIX_WIDTH = MIX_HEADS * HEAD_DIM
MEM_WIDTH = MEM_HEADS * HEAD_DIM
MLA_Q_RANK = 384
MLA_KV_RANK = 256
MLA_NOPE_DIM = 64
MLA_ROPE_DIM = 32
MLA_QK_DIM = MLA_NOPE_DIM + MLA_ROPE_DIM
ROPE_HALF = MLA_ROPE_DIM // 2
ROPE_BASE = 10000.0
RWKV_DECAY_RANK = 64
RWKV_A_RANK = 64
RWKV_GATE_RANK = 160
RWKV_GN_EPS = 64e-5
LN_EPS = 1e-5
RMS_EPS = 1e-6
QK_SCALE = MLA_QK_DIM ** -0.5
MEM_SCALE = HEAD_DIM ** -0.5
MASK_VALUE = -1e30

LANES = 128
HEADS_PER_TILE = LANES // HEAD_DIM
N_PAIRS = MIX_HEADS // HEADS_PER_TILE
VMEM_LIMIT_BYTES = 56 * 1024 * 1024

ROW_TILE = 512
ATTN_TILE = 256
SCAN_CHUNK = 64
SCAN_CHUNKS_PER_STEP = 4
FF_CHUNK = 1024

RW_HW = 3 * MIX_WIDTH
RW_HA = RW_HW + LANES
RW_HG = RW_HA + LANES
RW_TM_COLS = RW_HG + 2 * LANES
RW_COLS = RW_TM_COLS + MEM_WIDTH

ML_CKV = MLA_Q_RANK
ML_QMEM = ML_CKV + MLA_KV_RANK
ML_KR = ML_QMEM + MEM_WIDTH
ML_KR_SWAP = ML_KR + LANES
ML_COLS = ML_KR_SWAP + LANES


def _bdot(a, b):
    return jnp.dot(a.astype(BF16), b.astype(BF16), preferred_element_type=F32)


def _bdot_nt(a, b):
    return lax.dot_general(a.astype(BF16), b.astype(BF16), (((1,), (1,)), ((), ())),
                           preferred_element_type=F32)


def _split_dot(x, w):
    hi = x.astype(BF16)
    lo = (x - hi.astype(F32)).astype(BF16)
    return (jnp.dot(hi, w, preferred_element_type=F32) + jnp.dot(lo, w, preferred_element_type=F32))


def _layer_norm(t, g, b):
    mu = jnp.mean(t, axis=-1, keepdims=True)
    d = t - mu
    var = jnp.mean(d * d, axis=-1, keepdims=True)
    return d * lax.rsqrt(var + LN_EPS) * g + b


def _rms_norm(t, g):
    return t * lax.rsqrt(jnp.mean(t * t, axis=-1, keepdims=True) + RMS_EPS) * g


def _sigmoid(z):
    return 1.0 / (1.0 + jnp.exp(-z))


def _params(*semantics):
    return pltpu.CompilerParams(dimension_semantics=semantics, vmem_limit_bytes=VMEM_LIMIT_BYTES)


def _const_spec(shape):
    nd = len(shape)
    return pl.BlockSpec(shape, lambda *_: (0,) * nd)


def _mem_kv_kernel(mem_ref, g_ref, b_ref, w_ref, k_ref, v_ref):
    m_tokens = mem_ref.shape[1]
    kv = _bdot(_layer_norm(mem_ref[0], g_ref[...], b_ref[...]), w_ref[...])
    k = kv[:, :MEM_WIDTH]
    v = kv[:, MEM_WIDTH:]
    head_of_lane = jnp.right_shift(lax.broadcasted_iota(jnp.int32, k.shape, 1), HEAD_DIM.bit_length() - 1)
    for h in range(MEM_HEADS):
        rows = slice(h * m_tokens, (h + 1) * m_tokens)
        k_ref[0, rows, :] = jnp.where(head_of_lane == h, k, 0.0).astype(BF16)
        v_ref[0, rows, :] = jnp.where(head_of_lane == h, v, 0.0).astype(BF16)


def _mem_kv(mem, g, b, w):
    bsz, m_tokens, d = mem.shape
    out = jax.ShapeDtypeStruct((bsz, MEM_HEADS * m_tokens, MEM_WIDTH), BF16)
    blk = pl.BlockSpec((1, MEM_HEADS * m_tokens, MEM_WIDTH), lambda i: (i, 0, 0))
    return pl.pallas_call(
        _mem_kv_kernel,
        grid=(bsz,),
        in_specs=[pl.BlockSpec((1, m_tokens, d), lambda i: (i, 0, 0)),
                  _const_spec((1, d)), _const_spec((1, d)), _const_spec(w.shape)],
        out_specs=[blk, blk],
        out_shape=[out, out],
        compiler_params=_params("parallel"),
        name="mem_kv",
    )(mem, g.reshape(1, d), b.reshape(1, d), w)


def _mla_proj_kernel(x_ref, cos_ref, sin_ref, w_in_ref, qg_ref, wq_ref, kvg_ref, wkv_ref,
                     qn_ref, qr_ref, kn_ref, kr_ref, v_ref, qm_ref):
    c = _bdot(x_ref[...], w_in_ref[...])
    q = _bdot(_rms_norm(c[:, :ML_CKV], qg_ref[...]), wq_ref[...])
    kv = _bdot(_rms_norm(c[:, ML_CKV:ML_QMEM], kvg_ref[...]), wkv_ref[...])
    cos = cos_ref[...]
    sin = sin_ref[...]
    cos_all = jnp.concatenate([cos] * N_PAIRS, axis=1)
    sin_all = jnp.concatenate([sin] * N_PAIRS, axis=1)
    qn_ref[...] = (q[:, :MIX_WIDTH] * QK_SCALE).astype(BF16)
    q_rot = q[:, MIX_WIDTH:2 * MIX_WIDTH] * cos_all + q[:, 2 * MIX_WIDTH:] * sin_all
    qr_ref[...] = (q_rot * QK_SCALE).astype(BF16)
    kn_ref[...] = kv[:, :MIX_WIDTH].astype(BF16)
    v_ref[...] = kv[:, MIX_WIDTH:].astype(BF16)
    kr_ref[...] = (c[:, ML_KR:ML_KR_SWAP] * cos + c[:, ML_KR_SWAP:] * sin).astype(BF16)
    qm_ref[...] = (c[:, ML_QMEM:ML_KR] * MEM_SCALE).astype(BF16)


def _mla_proj(x2d, cos, sin, w_in, q_norm, w_q, kv_norm, w_kv):
    n, d = x2d.shape
    tm = min(ROW_TILE, n)
    row = lambda width: pl.BlockSpec((tm, width), lambda i: (i, 0))
    shape = lambda width: jax.ShapeDtypeStruct((n, width), BF16)
    return pl.pallas_call(
        _mla_proj_kernel,
        grid=(n // tm,),
        in_specs=[row(d), row(LANES), row(LANES), _const_spec(w_in.shape), _const_spec(q_norm.shape),
                  _const_spec(w_q.shape), _const_spec(kv_norm.shape), _const_spec(w_kv.shape)],
        out_specs=[row(MIX_WIDTH), row(MIX_WIDTH), row(MIX_WIDTH), row(LANES), row(MIX_WIDTH), row(MEM_WIDTH)],
        out_shape=[shape(MIX_WIDTH), shape(MIX_WIDTH), shape(MIX_WIDTH), shape(LANES), shape(MIX_WIDTH),
                   shape(MEM_WIDTH)],
        compiler_params=_params("parallel"),
        name="mla_proj",
    )(x2d, cos, sin, w_in, q_norm, w_q, kv_norm, w_kv)


def _mla_attn_kernel(qn_ref, qr_ref, kn_ref, kr_ref, v_ref, o_ref, q_sc, m_sc, l_sc, acc_sc):
    tq = qn_ref.shape[1]
    qi = pl.program_id(2)
    lane = lax.broadcasted_iota(jnp.int32, (tq, LANES), 1)
    qn = qn_ref[0]
    qr = qr_ref[0]
    zero = jnp.zeros_like(qn)
    q_sc[:tq, :LANES] = jnp.where(lane < HEAD_DIM, qn, zero)
    q_sc[:tq, LANES:] = jnp.where(lane < MLA_ROPE_DIM, qr, zero)
    q_sc[tq:, :LANES] = jnp.where(lane >= HEAD_DIM, qn, zero)
    q_sc[tq:, LANES:] = jnp.where(lane >= MLA_ROPE_DIM, qr, zero)
    m_sc[...] = jnp.full(m_sc.shape, -jnp.inf, F32)
    l_sc[...] = jnp.zeros(l_sc.shape, F32)
    acc_sc[...] = jnp.zeros(acc_sc.shape, F32)

    def step(j, masked):
        rows = pl.ds(pl.multiple_of(j * tq, tq), tq)
        k = jnp.concatenate([kn_ref[0, rows, :], kr_ref[0, rows, :]], axis=1)
        s = lax.dot_general(q_sc[...], k, (((1,), (1,)), ((), ())), preferred_element_type=F32)
        if masked:
            q_pos = jnp.bitwise_and(lax.broadcasted_iota(jnp.int32, s.shape, 0), tq - 1)
            k_pos = lax.broadcasted_iota(jnp.int32, s.shape, 1)
            s = jnp.where(k_pos <= q_pos, s, MASK_VALUE)
        m_prev = m_sc[...]
        m_new = jnp.maximum(m_prev, jnp.max(s, axis=1, keepdims=True))
        alpha = jnp.exp(m_prev - m_new)
        p = jnp.exp(s - m_new)
        l_sc[...] = alpha * l_sc[...] + jnp.sum(p, axis=1, keepdims=True)
        acc_sc[...] = alpha * acc_sc[...] + jnp.dot(p.astype(BF16), v_ref[0, rows, :],
                                                    preferred_element_type=F32)
        m_sc[...] = m_new

    def body(j, carry):
        step(j, False)
        return carry

    lax.fori_loop(0, qi, body, 0)
    step(qi, True)
    out = acc_sc[...] / l_sc[...]
    o_ref[0] = jnp.where(lane < HEAD_DIM, out[:tq], out[tq:]).astype(BF16)


def _mla_attn(qn, qr, kn, kr, v):
    bsz, t, _ = qn.shape
    tq = min(ATTN_TILE, t)
    assert tq & (tq - 1) == 0 and t % tq == 0
    q_spec = pl.BlockSpec((1, tq, LANES), lambda b, p, i: (b, i, p))
    seq_spec = pl.BlockSpec((1, t, LANES), lambda b, p, i: (b, 0, p))
    return pl.pallas_call(
        _mla_attn_kernel,
        grid=(bsz, N_PAIRS, t // tq),
        in_specs=[q_spec, q_spec, seq_spec, pl.BlockSpec((1, t, LANES), lambda b, p, i: (b, 0, 0)), seq_spec],
        out_specs=q_spec,
        out_shape=jax.ShapeDtypeStruct((bsz, t, MIX_WIDTH), BF16),
        scratch_shapes=[pltpu.VMEM((HEADS_PER_TILE * tq, 2 * LANES), BF16),
                        pltpu.VMEM((HEADS_PER_TILE * tq, 1), F32),
                        pltpu.VMEM((HEADS_PER_TILE * tq, 1), F32),
                        pltpu.VMEM((HEADS_PER_TILE * tq, LANES), F32)],
        compiler_params=_params("parallel", "parallel", "arbitrary"),
        name="mla_attn",
    )(qn, qr, kn, kr, v)


def _rwkv_proj_kernel(x_ref, w_in_ref, mu_ref, w0_ref, w2_ref, a0_ref, a2_ref, g2_ref, kk_ref, ka_ref,
                      rk_ref, seg_ref, r_o, lw_o, k_o, v_o, a_o, b_o, g_o, bonus_o, qm_o, carry_sc):
    tm = x_ref.shape[1]

    @pl.when(pl.program_id(1) == 0)
    def _():
        carry_sc[...] = jnp.zeros(carry_sc.shape, F32)

    proj = _bdot(x_ref[0], w_in_ref[...])
    prev = carry_sc[7:8, :]
    carry_sc[...] = proj[tm - 8:, :RW_TM_COLS]
    first_row = lax.broadcasted_iota(jnp.int32, (tm, 1), 0) == 0

    def mixed(lo, hi):
        cur = proj[:, lo:hi]
        shifted = jnp.where(first_row, prev[:, lo:hi], pltpu.roll(cur, 1, 0))
        return cur + (shifted - cur) * mu_ref[:, lo:hi]

    r = mixed(0, MIX_WIDTH)
    k = mixed(MIX_WIDTH, 2 * MIX_WIDTH)
    v = mixed(2 * MIX_WIDTH, RW_HW)
    hw = mixed(RW_HW, RW_HA)
    ha = mixed(RW_HA, RW_HG)
    hg = mixed(RW_HG, RW_TM_COLS)

    z = -(w0_ref[...] + _bdot(jnp.tanh(hw), w2_ref[...]))
    softplus = jnp.maximum(z, 0.0) + jnp.log(1.0 + jnp.exp(-jnp.abs(z)))
    log_decay = -jnp.exp(-softplus - 0.5)
    a = _sigmoid(a0_ref[...] + _bdot(ha, a2_ref[...]))
    gate = _bdot(_sigmoid(hg), g2_ref[...])

    seg = seg_ref[...]
    kk = k * kk_ref[...]
    kk = kk / jnp.maximum(jnp.sqrt(_split_dot(kk * kk, seg)), 1e-12)
    k2 = k * (1.0 + (a - 1.0) * ka_ref[...])
    bonus = _split_dot(r * k2 * rk_ref[...], seg) * v

    r_o[0] = r
    lw_o[0] = log_decay
    k_o[0] = k2
    v_o[0] = v
    a_o[0] = -kk
    b_o[0] = kk * a
    g_o[0] = gate
    bonus_o[0] = bonus
    qm_o[0] = (proj[:, RW_TM_COLS:] * MEM_SCALE).astype(BF16)


def _rwkv_proj(x, w_in, mu, w0, w2, a0, a2, g2, k_k, k_a, r_k, seg):
    bsz, t, d = x.shape
    tm = min(ROW_TILE // 2, t)
    row = lambda width: pl.BlockSpec((1, tm, width), lambda b, i: (b, i, 0))
    wide = jax.ShapeDtypeStruct((bsz, t, MIX_WIDTH), F32)
    consts = (w_in, mu, w0, w2, a0, a2, g2, k_k, k_a, r_k, seg)
    return pl.pallas_call(
        _rwkv_proj_kernel,
        grid=(bsz, t // tm),
        in_specs=[row(d)] + [_const_spec(c.shape) for c in consts],
        out_specs=[row(MIX_WIDTH)] * 8 + [row(MEM_WIDTH)],
        out_shape=[wide] * 8 + [jax.ShapeDtypeStruct((bsz, t, MEM_WIDTH), BF16)],
        scratch_shapes=[pltpu.VMEM((8, RW_TM_COLS), F32)],
        compiler_params=_params("parallel", "arbitrary"),
        name="rwkv_proj",
    )(x, *consts)


def _scan_chunk(state, r, lw, k, v, a, b):
    c = r.shape[0]
    lane = lax.broadcasted_iota(jnp.int32, (c, LANES), 1)
    head0 = lane < HEAD_DIM

    def stack(z):
        return jnp.concatenate([jnp.where(head0, z, 0.0), jnp.where(head0, 0.0, z)], axis=0)

    ti = lax.broadcasted_iota(jnp.int32, (c, c), 0)
    tj = lax.broadcasted_iota(jnp.int32, (c, c), 1)
    tril_ones = jnp.where(ti >= tj, 1.0, 0.0).astype(BF16)
    lw_hi = lw.astype(BF16)
    lw_mid = (lw - lw_hi.astype(F32)).astype(BF16)
    lw_lo = (lw - lw_hi.astype(F32) - lw_mid.astype(F32)).astype(BF16)
    cum = (jnp.dot(tril_ones, lw_hi, preferred_element_type=F32)
           + jnp.dot(tril_ones, lw_mid, preferred_element_type=F32)
           + jnp.dot(tril_ones, lw_lo, preferred_element_type=F32))
    w_incl = jnp.exp(cum)
    w_inv = jnp.exp(-cum)
    w_excl = jnp.exp(cum - lw)
    rs = stack(r * w_incl)
    as_ = stack(a * w_excl)
    bs = stack(b * w_inv)
    ks = stack(k * w_inv)
    vs = stack(v)
    w_end = w_incl[c - 1:c, :]

    n = HEADS_PER_TILE * c
    bk = jnp.concatenate([bs, ks], axis=0).astype(BF16)
    m = _bdot_nt(jnp.concatenate([as_, rs], axis=0), bk)
    i2 = lax.broadcasted_iota(jnp.int32, (n, n), 0)
    j2 = lax.broadcasted_iota(jnp.int32, (n, n), 1)
    strict = i2 > j2
    incl = i2 >= j2
    a_ab = jnp.where(strict, m[:n, :n], 0.0)
    a_ak = jnp.where(strict, m[:n, n:], 0.0)
    p_rb = jnp.where(incl, m[n:, :n], 0.0)
    p_rk = jnp.where(incl, m[n:, n:], 0.0)

    x = jnp.concatenate([as_, _bdot(a_ak, vs)], axis=1)
    a_pow = a_ab
    reach = 1
    while True:
        x = x + _bdot(a_pow, x)
        reach *= 2
        if reach >= c:
            break
        a_pow = _bdot(a_pow, a_pow)
    z = _bdot(p_rb, x)
    r_eff = rs + z[:, :LANES]
    y_loc = z[:, LANES:] + _bdot(p_rk, vs)
    ys = _bdot_nt(jnp.concatenate([r_eff, x[:, :LANES]], axis=0), state)
    y = ys[:n] + y_loc
    u = ys[n:] + x[:, LANES:]
    uv = jnp.concatenate([u, vs], axis=0)
    new_state = (state + _bdot(uv.T, bk)) * w_end
    return y[:c] + y[c:], new_state


def _rwkv_scan_kernel(r_ref, lw_ref, k_ref, v_ref, a_ref, b_ref, y_ref, state_sc):
    @pl.when(pl.program_id(2) == 0)
    def _():
        state_sc[...] = jnp.zeros(state_sc.shape, F32)

    state = state_sc[...]
    for ci in range(r_ref.shape[1] // SCAN_CHUNK):
        rows = slice(ci * SCAN_CHUNK, (ci + 1) * SCAN_CHUNK)
        y, state = _scan_chunk(state, r_ref[0, rows, :], lw_ref[0, rows, :], k_ref[0, rows, :],
                               v_ref[0, rows, :], a_ref[0, rows, :], b_ref[0, rows, :])
        y_ref[0, rows, :] = y
    state_sc[...] = state


def _rwkv_scan(r, lw, k, v, a, b):
    bsz, t, _ = r.shape
    tc = min(SCAN_CHUNK * SCAN_CHUNKS_PER_STEP, t)
    spec = pl.BlockSpec((1, tc, LANES), lambda bi, p, c: (bi, c, p))
    return pl.pallas_call(
        _rwkv_scan_kernel,
        grid=(bsz, N_PAIRS, t // tc),
        in_specs=[spec] * 6,
        out_specs=spec,
        out_shape=jax.ShapeDtypeStruct((bsz, t, MIX_WIDTH), F32),
        scratch_shapes=[pltpu.VMEM((LANES, LANES), F32)],
        compiler_params=_params("parallel", "parallel", "arbitrary"),
        name="rwkv_scan",
    )(r, lw, k, v, a, b)


def _tail_kernel(*refs, rwkv, alpha):
    if rwkv:
        y_ref, bonus_ref, gate_ref, gng_ref, gnb_ref, seg_ref = refs[:6]
        refs = refs[6:]
    else:
        y_ref = refs[0]
        refs = refs[1:]
    (qm_ref, mk_ref, mv_ref, x_ref, wo_mix_ref, wo_mem_ref, ln1g_ref, ln1b_ref,
     w1_ref, w2_ref, ln2g_ref, ln2b_ref, o_ref) = refs

    if rwkv:
        y = y_ref[0]
        seg = seg_ref[...]
        mean = _split_dot(y, seg) * (1.0 / HEAD_DIM)
        d = y - mean
        var = _split_dot(d * d, seg) * (1.0 / HEAD_DIM)
        yn = d * lax.rsqrt(var + RWKV_GN_EPS) * gng_ref[...] + gnb_ref[...]
        o_mix = ((yn + bonus_ref[0]) * gate_ref[0]).astype(BF16)
    else:
        o_mix = y_ref[0]

    m_tokens = mk_ref.shape[1] // MEM_HEADS
    s = lax.dot_general(qm_ref[0], mk_ref[0], (((1,), (1,)), ((), ())), preferred_element_type=F32)
    probs = []
    for h in range(MEM_HEADS):
        sh = s[:, h * m_tokens:(h + 1) * m_tokens]
        e = jnp.exp(sh - jnp.max(sh, axis=1, keepdims=True))
        probs.append((e / jnp.sum(e, axis=1, keepdims=True)).astype(BF16))
    o_mem = jnp.dot(jnp.concatenate(probs, axis=1), mv_ref[0], preferred_element_type=F32)

    mix = (jnp.dot(o_mix, wo_mix_ref[...], preferred_element_type=F32)
           + _bdot(o_mem, wo_mem_ref[...]))
    x1 = _layer_norm(alpha * x_ref[0] + mix, ln1g_ref[...], ln1b_ref[...])
    x1b = x1.astype(BF16)
    ff = jnp.zeros(x1.shape, F32)
    for c in range(w1_ref.shape[1] // FF_CHUNK):
        cols = slice(c * FF_CHUNK, (c + 1) * FF_CHUNK)
        hidden = jnp.dot(x1b, w1_ref[:, cols], preferred_element_type=F32)
        hidden = jnp.square(jnp.maximum(hidden, 0.0)).astype(BF16)
        ff = ff + jnp.dot(hidden, w2_ref[cols, :], preferred_element_type=F32)
    o_ref[0] = _layer_norm(alpha * x1 + ff, ln2g_ref[...], ln2b_ref[...])


def _tail(mix_inputs, qm, mem_k, mem_v, x, wo_mix, wo_mem, ln1g, ln1b, w1, w2, ln2g, ln2b, *, rwkv, alpha):
    bsz, t, d = x.shape
    tm = min(ROW_TILE, t)
    row = lambda width: pl.BlockSpec((1, tm, width), lambda b, i: (b, i, 0))
    mem_spec = pl.BlockSpec((1,) + mem_k.shape[1:], lambda b, i: (b, 0, 0))
    single = lambda a: pl.BlockSpec(a.shape, lambda b, i: (0,) * a.ndim, pipeline_mode=pl.Buffered(1))
    if rwkv:
        y, bonus, gate, gng, gnb, seg = mix_inputs
        head_args = [y, bonus, gate, gng, gnb, seg]
        head_specs = [row(MIX_WIDTH)] * 3 + [single(gng), single(gnb), single(seg)]
    else:
        head_args = list(mix_inputs)
        head_specs = [row(MIX_WIDTH)]
    consts = (wo_mix, wo_mem, ln1g, ln1b, w1, w2, ln2g, ln2b)
    return pl.pallas_call(
        functools.partial(_tail_kernel, rwkv=rwkv, alpha=alpha),
        grid=(bsz, t // tm),
        in_specs=head_specs + [row(MEM_WIDTH), mem_spec, mem_spec, row(d)] + [single(c) for c in consts],
        out_specs=row(d),
        out_shape=jax.ShapeDtypeStruct((bsz, t, d), F32),
        compiler_params=_params("parallel", "parallel"),
        name="rwkv_tail" if rwkv else "mla_tail",
    )(*head_args, qm, mem_k, mem_v, x, *consts)


def _swap_halves(w):
    half = w.shape[-1] // 2
    return jnp.concatenate([w[..., half:], w[..., :half]], axis=-1)


def _pair_pack(w):
    kdim = w.shape[0]
    w = w.reshape(kdim, N_PAIRS, HEADS_PER_TILE * MLA_ROPE_DIM)
    pad = jnp.zeros((kdim, N_PAIRS, LANES - HEADS_PER_TILE * MLA_ROPE_DIM), w.dtype)
    return jnp.concatenate([w, pad], axis=-1).reshape(kdim, N_PAIRS * LANES)


def _mla_weights(w_in, w_q_up, w_kv_up):
    d = w_in.shape[0]
    o_kv, o_kr, o_qm = MLA_Q_RANK, MLA_Q_RANK + MLA_KV_RANK, MLA_Q_RANK + MLA_KV_RANK + MLA_ROPE_DIM
    w_kr = w_in[:, o_kr:o_qm]
    pad = jnp.zeros((d, LANES - 2 * MLA_ROPE_DIM), w_in.dtype)
    w_in_p = jnp.concatenate([w_in[:, :o_kr], w_in[:, o_qm:], w_kr, w_kr, pad,
                              _swap_halves(w_kr), _swap_halves(w_kr), pad], axis=1)
    wq = w_q_up.reshape(MLA_Q_RANK, MIX_HEADS, MLA_QK_DIM)
    rope = wq[:, :, MLA_NOPE_DIM:]
    w_q = jnp.concatenate([wq[:, :, :MLA_NOPE_DIM].reshape(MLA_Q_RANK, MIX_WIDTH),
                           _pair_pack(rope), _pair_pack(_swap_halves(rope))], axis=1)
    wkv = w_kv_up.reshape(MLA_KV_RANK, MIX_HEADS, 2 * HEAD_DIM)
    w_kv = jnp.concatenate([wkv[:, :, :HEAD_DIM].reshape(MLA_KV_RANK, MIX_WIDTH),
                            wkv[:, :, HEAD_DIM:].reshape(MLA_KV_RANK, MIX_WIDTH)], axis=1)
    return w_in_p.astype(BF16), w_q.astype(BF16), w_kv.astype(BF16)


def _rope_tables(positions):
    inv_freq = ROPE_BASE ** (-jnp.arange(ROPE_HALF, dtype=F32) * 2.0 / MLA_ROPE_DIM)
    ang = positions.astype(F32).reshape(-1, 1) * inv_freq
    cos, sin = jnp.cos(ang), jnp.sin(ang)
    pad = jnp.zeros((ang.shape[0], LANES - 2 * MLA_ROPE_DIM), F32)
    return (jnp.concatenate([cos, cos, cos, cos, pad], axis=1),
            jnp.concatenate([-sin, sin, -sin, sin, pad], axis=1))


def _pad_cols(w, width):
    return jnp.concatenate([w, jnp.zeros(w.shape[:-1] + (width - w.shape[-1],), w.dtype)], axis=-1)


def _pad_rows(w, height):
    return jnp.concatenate([w, jnp.zeros((height - w.shape[0],) + w.shape[1:], w.dtype)], axis=0)


def _rwkv_weights(w_in, mu, w2, a2, g2):
    o_hw, o_ha = 3 * MIX_WIDTH, 3 * MIX_WIDTH + RWKV_DECAY_RANK
    o_hg = o_ha + RWKV_A_RANK
    o_qm = o_hg + RWKV_GATE_RANK

    def layout(w):
        return jnp.concatenate([w[..., :o_hw], _pad_cols(w[..., o_hw:o_ha], LANES),
                                _pad_cols(w[..., o_ha:o_hg], LANES),
                                _pad_cols(w[..., o_hg:o_qm], 2 * LANES)], axis=-1)

    w_in_p = jnp.concatenate([layout(w_in[:, :o_qm]), w_in[:, o_qm:]], axis=1)
    return (w_in_p.astype(BF16), layout(mu).reshape(1, RW_TM_COLS),
            _pad_rows(w2, LANES).astype(BF16), _pad_rows(a2, LANES).astype(BF16),
            _pad_rows(g2, 2 * LANES).astype(BF16))


def kernel(x, mem, positions, mem_ln_g, mem_ln_b, w_mem_kv, mla_w_in, mla_q_norm, mla_w_q_up, mla_kv_norm, mla_w_kv_up, rwkv_w_in, rwkv_mu, rwkv_w0, rwkv_w2, rwkv_a0, rwkv_a2, rwkv_g2, rwkv_k_k, rwkv_k_a, rwkv_r_k, rwkv_gn_g, rwkv_gn_b, w_out, ln1_g, ln1_b, w_ff1, w_ff2, ln2_g, ln2_b):
    bsz, t, d = x.shape
    depth = w_out.shape[0]
    alpha = (2.0 * depth) ** 0.25
    row = lambda p: p.reshape(1, -1)

    mem_k, mem_v = _mem_kv(mem, mem_ln_g, mem_ln_b, w_mem_kv.astype(BF16))
    cos, sin = _rope_tables(positions)
    head_of = jnp.arange(MIX_WIDTH) // HEAD_DIM
    seg = (head_of[:, None] == head_of[None, :]).astype(BF16)

    for i in range(depth):
        j = i // 2
        tail_w = (w_out[i, :MIX_WIDTH].astype(BF16), w_out[i, MIX_WIDTH:].astype(BF16),
                  row(ln1_g[i]), row(ln1_b[i]), w_ff1[i].astype(BF16), w_ff2[i].astype(BF16),
                  row(ln2_g[i]), row(ln2_b[i]))
        if i % 2 == 0:
            w_in, w_q, w_kv = _mla_weights(mla_w_in[j], mla_w_q_up[j], mla_w_kv_up[j])
            qn, qr, kn, kr, v, qm = _mla_proj(x.reshape(bsz * t, d), cos, sin, w_in, row(mla_q_norm[j]), w_q,
                                              row(mla_kv_norm[j]), w_kv)
            seq = lambda a: a.reshape(bsz, t, a.shape[-1])
            o_mix = _mla_attn(seq(qn), seq(qr), seq(kn), seq(kr), seq(v))
            x = _tail((o_mix,), seq(qm), mem_k, mem_v, x, *tail_w, rwkv=False, alpha=alpha)
        else:
            w_in, mu, w2, a2, g2 = _rwkv_weights(rwkv_w_in[j], rwkv_mu[j], rwkv_w2[j], rwkv_a2[j], rwkv_g2[j])
            r, lw, k, v, a, b, gate, bonus, qm = _rwkv_proj(
                x, w_in, mu, row(rwkv_w0[j]), w2, row(rwkv_a0[j]), a2, g2, row(rwkv_k_k[j]), row(rwkv_k_a[j]),
                row(rwkv_r_k[j]), seg)
            y = _rwkv_scan(r, lw, k, v, a, b)
            x = _tail((y, bonus, gate, row(rwkv_gn_g[j]), row(rwkv_gn_b[j]), seg), qm, mem_k, mem_v, x,
                      *tail_w, rwkv=True, alpha=alpha)
    return x
```

```python
import functools

import jax
import jax.numpy as jnp
from jax import lax
from jax.experimental import pallas as pl
from jax.experimental.pallas import tpu as pltpu

F32 = jnp.float32
BF16 = jnp.bfloat16

HEAD_DIM = 64
MIX_HEADS = 12
MEM_HEADS = 4
MIX_WIDTH = MIX_HEADS * HEAD_DIM
MEM_WIDTH = MEM_HEADS * HEAD_DIM
MLA_Q_RANK = 384
MLA_KV_RANK = 256
MLA_NOPE_DIM = 64
MLA_ROPE_DIM = 32
MLA_QK_DIM = MLA_NOPE_DIM + MLA_ROPE_DIM
ROPE_HALF = MLA_ROPE_DIM // 2
ROPE_BASE = 10000.0
RWKV_DECAY_RANK = 64
RWKV_A_RANK = 64
RWKV_GATE_RANK = 160
RWKV_GN_EPS = 64e-5
LN_EPS = 1e-5
RMS_EPS = 1e-6
QK_SCALE = MLA_QK_DIM ** -0.5 * 1.4426950408889634
MEM_SCALE = HEAD_DIM ** -0.5
MASK_VALUE = -1e30

LANES = 128
HEADS_PER_TILE = LANES // HEAD_DIM
N_PAIRS = MIX_HEADS // HEADS_PER_TILE
VMEM_LIMIT_BYTES = 56 * 1024 * 1024

ROW_TILE = 512
ATTN_TILE = 512
SCAN_CHUNK = 64
SCAN_CHUNKS_PER_STEP = 8
FF_CHUNK = 1024

RW_HW = 3 * MIX_WIDTH
RW_HA = RW_HW + LANES
RW_HG = RW_HA + LANES
RW_TM_COLS = RW_HG + 2 * LANES
RW_COLS = RW_TM_COLS + MEM_WIDTH

ML_CKV = MLA_Q_RANK
ML_QMEM = ML_CKV + MLA_KV_RANK
ML_KR = ML_QMEM + MEM_WIDTH
ML_KR_SWAP = ML_KR + LANES
ML_COLS = ML_KR_SWAP + LANES


def _bdot(a, b):
    return jnp.dot(a.astype(BF16), b.astype(BF16), preferred_element_type=F32)


def _bdot_nt(a, b):
    return lax.dot_general(a.astype(BF16), b.astype(BF16), (((1,), (1,)), ((), ())),
                           preferred_element_type=F32)


def _split_dot(x, w):
    hi = x.astype(BF16)
    lo = (x - hi.astype(F32)).astype(BF16)
    return (jnp.dot(hi, w, preferred_element_type=F32) + jnp.dot(lo, w, preferred_element_type=F32))


def _layer_norm(t, g, b):
    mu = jnp.mean(t, axis=-1, keepdims=True)
    d = t - mu
    var = jnp.mean(d * d, axis=-1, keepdims=True)
    return d * lax.rsqrt(var + LN_EPS) * g + b


def _rms_norm(t, g):
    return t * lax.rsqrt(jnp.mean(t * t, axis=-1, keepdims=True) + RMS_EPS) * g


def _sigmoid(z):
    return 1.0 / (1.0 + jnp.exp(-z))


def _params(*semantics):
    return pltpu.CompilerParams(dimension_semantics=semantics, vmem_limit_bytes=VMEM_LIMIT_BYTES)


def _const_spec(shape):
    nd = len(shape)
    return pl.BlockSpec(shape, lambda *_: (0,) * nd)


def _mem_kv_kernel(mem_ref, g_ref, b_ref, w_ref, k_ref, v_ref):
    m_tokens = mem_ref.shape[1]
    kv = _bdot(_layer_norm(mem_ref[0], g_ref[...], b_ref[...]), w_ref[...])
    k = kv[:, :MEM_WIDTH]
    v = kv[:, MEM_WIDTH:]
    head_of_lane = jnp.right_shift(lax.broadcasted_iota(jnp.int32, k.shape, 1), HEAD_DIM.bit_length() - 1)
    for h in range(MEM_HEADS):
        rows = slice(h * m_tokens, (h + 1) * m_tokens)
        k_ref[0, rows, :] = jnp.where(head_of_lane == h, k, 0.0).astype(BF16)
        v_ref[0, rows, :] = jnp.where(head_of_lane == h, v, 0.0).astype(BF16)


def _mem_kv(mem, g, b, w):
    bsz, m_tokens, d = mem.shape
    out = jax.ShapeDtypeStruct((bsz, MEM_HEADS * m_tokens, MEM_WIDTH), BF16)
    blk = pl.BlockSpec((1, MEM_HEADS * m_tokens, MEM_WIDTH), lambda i: (i, 0, 0))
    return pl.pallas_call(
        _mem_kv_kernel,
        grid=(bsz,),
        in_specs=[pl.BlockSpec((1, m_tokens, d), lambda i: (i, 0, 0)),
                  _const_spec((1, d)), _const_spec((1, d)), _const_spec(w.shape)],
        out_specs=[blk, blk],
        out_shape=[out, out],
        compiler_params=_params("parallel"),
        name="mem_kv",
    )(mem, g.reshape(1, d), b.reshape(1, d), w)


def _mla_proj_kernel(x_ref, cos_ref, sin_ref, w_in_ref, qg_ref, wq_ref, kvg_ref, wkv_ref,
                     qn_ref, qr_ref, kn_ref, kr_ref, v_ref, qm_ref):
    c = _bdot(x_ref[...], w_in_ref[...])
    q = _bdot(_rms_norm(c[:, :ML_CKV], qg_ref[...]), wq_ref[...])
    kv = _bdot(_rms_norm(c[:, ML_CKV:ML_QMEM], kvg_ref[...]), wkv_ref[...])
    cos = cos_ref[...]
    sin = sin_ref[...]
    cos_all = jnp.concatenate([cos] * N_PAIRS, axis=1)
    sin_all = jnp.concatenate([sin] * N_PAIRS, axis=1)
    qn_ref[...] = (q[:, :MIX_WIDTH] * QK_SCALE).astype(BF16)
    q_rot = q[:, MIX_WIDTH:2 * MIX_WIDTH] * cos_all + q[:, 2 * MIX_WIDTH:] * sin_all
    qr_ref[...] = (q_rot * QK_SCALE).astype(BF16)
    kn_ref[...] = kv[:, :MIX_WIDTH].astype(BF16)
    v_ref[...] = kv[:, MIX_WIDTH:].astype(BF16)
    kr_ref[...] = (c[:, ML_KR:ML_KR_SWAP] * cos + c[:, ML_KR_SWAP:] * sin).astype(BF16)
    qm_ref[...] = (c[:, ML_QMEM:ML_KR] * MEM_SCALE).astype(BF16)


def _mla_proj(x2d, cos, sin, w_in, q_norm, w_q, kv_norm, w_kv):
    n, d = x2d.shape
    tm = min(ROW_TILE, n)
    row = lambda width: pl.BlockSpec((tm, width), lambda i: (i, 0))
    shape = lambda width: jax.ShapeDtypeStruct((n, width), BF16)
    return pl.pallas_call(
        _mla_proj_kernel,
        grid=(n // tm,),
        in_specs=[row(d), row(LANES), row(LANES), _const_spec(w_in.shape), _const_spec(q_norm.shape),
                  _const_spec(w_q.shape), _const_spec(kv_norm.shape), _const_spec(w_kv.shape)],
        out_specs=[row(MIX_WIDTH), row(MIX_WIDTH), row(MIX_WIDTH), row(LANES), row(MIX_WIDTH), row(MEM_WIDTH)],
        out_shape=[shape(MIX_WIDTH), shape(MIX_WIDTH), shape(MIX_WIDTH), shape(LANES), shape(MIX_WIDTH),
                   shape(MEM_WIDTH)],
        compiler_params=_params("parallel"),
        name="mla_proj",
    )(x2d, cos, sin, w_in, q_norm, w_q, kv_norm, w_kv)


def _mla_attn_kernel(qn_ref, qr_ref, kn_ref, kr_ref, v_ref, o_ref, q_sc, m_sc, acc_sc):
    tq = qn_ref.shape[1]
    qi = pl.program_id(2)
    lane = lax.broadcasted_iota(jnp.int32, (tq, LANES), 1)
    qn = qn_ref[0]
    qr = qr_ref[0]
    zero = jnp.zeros_like(qn)
    q_sc[:tq, :LANES] = jnp.where(lane < HEAD_DIM, qn, zero)
    q_sc[:tq, LANES:] = jnp.where(lane < MLA_ROPE_DIM, qr, zero)
    q_sc[tq:, :LANES] = jnp.where(lane >= HEAD_DIM, qn, zero)
    q_sc[tq:, LANES:] = jnp.where(lane >= MLA_ROPE_DIM, qr, zero)
    m_sc[...] = jnp.full(m_sc.shape, -jnp.inf, F32)
    acc_sc[...] = jnp.zeros(acc_sc.shape, F32)
    ones = jnp.ones((tq, LANES), BF16)

    def step(j, masked):
        rows = pl.ds(pl.multiple_of(j * tq, tq), tq)
        k = jnp.concatenate([kn_ref[0, rows, :], kr_ref[0, rows, :]], axis=1)
        s = lax.dot_general(q_sc[...], k, (((1,), (1,)), ((), ())), preferred_element_type=F32)
        if masked:
            q_pos = jnp.bitwise_and(lax.broadcasted_iota(jnp.int32, s.shape, 0), tq - 1)
            k_pos = lax.broadcasted_iota(jnp.int32, s.shape, 1)
            s = jnp.where(k_pos <= q_pos, s, MASK_VALUE)
        m_prev = m_sc[...]
        m_new = jnp.maximum(m_prev, jnp.max(s, axis=1, keepdims=True))
        alpha = jnp.exp2(m_prev - m_new)
        p = jnp.exp2(s - jnp.concatenate([m_new] * (tq // LANES), axis=1))
        v_ones = jnp.concatenate([v_ref[0, rows, :], ones], axis=1)
        acc_sc[...] = (jnp.concatenate([alpha, alpha], axis=1) * acc_sc[...]
                       + jnp.dot(p.astype(BF16), v_ones, preferred_element_type=F32))
        m_sc[...] = m_new

    def body(j, carry):
        step(j, False)
        return carry

    lax.fori_loop(0, qi, body, 0)
    step(qi, True)
    out = acc_sc[:, :LANES] / acc_sc[:, LANES:]
    o_ref[0] = jnp.where(lane < HEAD_DIM, out[:tq], out[tq:]).astype(BF16)


def _mla_attn(qn, qr, kn, kr, v):
    bsz, t, _ = qn.shape
    tq = min(ATTN_TILE, t)
    assert tq & (tq - 1) == 0 and t % tq == 0
    q_spec = pl.BlockSpec((1, tq, LANES), lambda b, p, i: (b, i, p))
    seq_spec = pl.BlockSpec((1, t, LANES), lambda b, p, i: (b, 0, p))
    return pl.pallas_call(
        _mla_attn_kernel,
        grid=(bsz, N_PAIRS, t // tq),
        in_specs=[q_spec, q_spec, seq_spec, pl.BlockSpec((1, t, LANES), lambda b, p, i: (b, 0, 0)), seq_spec],
        out_specs=q_spec,
        out_shape=jax.ShapeDtypeStruct((bsz, t, MIX_WIDTH), BF16),
        scratch_shapes=[pltpu.VMEM((HEADS_PER_TILE * tq, 2 * LANES), BF16),
                        pltpu.VMEM((HEADS_PER_TILE * tq, LANES), F32),
                        pltpu.VMEM((HEADS_PER_TILE * tq, 2 * LANES), F32)],
        compiler_params=_params("parallel", "parallel", "arbitrary"),
        name="mla_attn",
    )(qn, qr, kn, kr, v)


def _rwkv_proj_kernel(x_ref, w_in_ref, mu_ref, w0_ref, w2_ref, a0_ref, a2_ref, g2_ref, kk_ref, ka_ref,
                      rk_ref, seg_ref, r_o, lw_o, k_o, v_o, a_o, b_o, g_o, bonus_o, qm_o, carry_sc):
    tm = x_ref.shape[1]

    @pl.when(pl.program_id(1) == 0)
    def _():
        carry_sc[...] = jnp.zeros(carry_sc.shape, F32)

    proj = _bdot(x_ref[0], w_in_ref[...])
    prev = carry_sc[7:8, :]
    carry_sc[...] = proj[tm - 8:, :RW_TM_COLS]
    first_row = lax.broadcasted_iota(jnp.int32, (tm, 1), 0) == 0

    def mixed(lo, hi):
        cur = proj[:, lo:hi]
        shifted = jnp.where(first_row, prev[:, lo:hi], pltpu.roll(cur, 1, 0))
        return cur + (shifted - cur) * mu_ref[:, lo:hi]

    r = mixed(0, MIX_WIDTH)
    k = mixed(MIX_WIDTH, 2 * MIX_WIDTH)
    v = mixed(2 * MIX_WIDTH, RW_HW)
    hw = mixed(RW_HW, RW_HA)
    ha = mixed(RW_HA, RW_HG)
    hg = mixed(RW_HG, RW_TM_COLS)

    z = -(w0_ref[...] + _bdot(jnp.tanh(hw), w2_ref[...]))
    softplus = jnp.maximum(z, 0.0) + jnp.log(1.0 + jnp.exp(-jnp.abs(z)))
    log_decay = -jnp.exp(-softplus - 0.5)
    a = _sigmoid(a0_ref[...] + _bdot(ha, a2_ref[...]))
    gate = _bdot(_sigmoid(hg), g2_ref[...])

    seg = seg_ref[...]
    kk = k * kk_ref[...]
    kk = kk / jnp.maximum(jnp.sqrt(_split_dot(kk * kk, seg)), 1e-12)
    k2 = k * (1.0 + (a - 1.0) * ka_ref[...])
    bonus = _split_dot(r * k2 * rk_ref[...], seg) * v

    r_o[0] = r
    lw_o[0] = log_decay
    k_o[0] = k2
    v_o[0] = v
    a_o[0] = -kk
    b_o[0] = kk * a
    g_o[0] = gate
    bonus_o[0] = bonus
    qm_o[0] = (proj[:, RW_TM_COLS:] * MEM_SCALE).astype(BF16)


def _rwkv_proj(x, w_in, mu, w0, w2, a0, a2, g2, k_k, k_a, r_k, seg):
    bsz, t, d = x.shape
    tm = min(ROW_TILE // 2, t)
    row = lambda width: pl.BlockSpec((1, tm, width), lambda b, i: (b, i, 0))
    wide = jax.ShapeDtypeStruct((bsz, t, MIX_WIDTH), F32)
    consts = (w_in, mu, w0, w2, a0, a2, g2, k_k, k_a, r_k, seg)
    return pl.pallas_call(
        _rwkv_proj_kernel,
        grid=(bsz, t // tm),
        in_specs=[row(d)] + [_const_spec(c.shape) for c in consts],
        out_specs=[row(MIX_WIDTH)] * 8 + [row(MEM_WIDTH)],
        out_shape=[wide] * 8 + [jax.ShapeDtypeStruct((bsz, t, MEM_WIDTH), BF16)],
        scratch_shapes=[pltpu.VMEM((8, RW_TM_COLS), F32)],
        compiler_params=_params("parallel", "arbitrary"),
        name="rwkv_proj",
    )(x, *consts)


def _rwkv_scan_kernel(r_ref, lw_ref, k_ref, v_ref, a_ref, b_ref, y_ref, state_sc):
    c = SCAN_CHUNK
    n = HEADS_PER_TILE * c
    chunks = range(r_ref.shape[1] // c)

    @pl.when(pl.program_id(2) == 0)
    def _():
        state_sc[...] = jnp.zeros(state_sc.shape, F32)

    head0 = lax.broadcasted_iota(jnp.int32, (c, LANES), 1) < HEAD_DIM

    def stack(z):
        return jnp.concatenate([jnp.where(head0, z, 0.0), jnp.where(head0, 0.0, z)], axis=0)

    tril_ones = jnp.where(lax.broadcasted_iota(jnp.int32, (c, c), 0)
                          >= lax.broadcasted_iota(jnp.int32, (c, c), 1), 1.0, 0.0).astype(BF16)
    i2 = lax.broadcasted_iota(jnp.int32, (n, n), 0)
    j2 = lax.broadcasted_iota(jnp.int32, (n, n), 1)
    strict = i2 > j2
    incl = i2 >= j2

    rs, as_, bk, vs, w_end = [], [], [], [], []
    for ci in chunks:
        rows = slice(ci * c, (ci + 1) * c)
        lw = lw_ref[0, rows, :]
        lw_hi = lw.astype(BF16)
        lw_mid = (lw - lw_hi.astype(F32)).astype(BF16)
        lw_lo = (lw - lw_hi.astype(F32) - lw_mid.astype(F32)).astype(BF16)
        cum = (jnp.dot(tril_ones, lw_hi, preferred_element_type=F32)
               + jnp.dot(tril_ones, lw_mid, preferred_element_type=F32)
               + jnp.dot(tril_ones, lw_lo, preferred_element_type=F32))
        w_incl = jnp.exp(cum)
        w_inv = jnp.exp(-cum)
        rs.append(stack(r_ref[0, rows, :] * w_incl))
        as_.append(stack(a_ref[0, rows, :] * jnp.exp(cum - lw)))
        bk.append(jnp.concatenate([stack(b_ref[0, rows, :] * w_inv), stack(k_ref[0, rows, :] * w_inv)],
                                  axis=0).astype(BF16))
        vs.append(stack(v_ref[0, rows, :]))
        w_end.append(w_incl[c - 1:c, :])

    m = [_bdot_nt(jnp.concatenate([as_[ci], rs[ci]], axis=0), bk[ci]) for ci in chunks]
    a_pow = [jnp.where(strict, m[ci][:n, :n], 0.0) for ci in chunks]
    x = [jnp.concatenate([as_[ci], _bdot(jnp.where(strict, m[ci][:n, n:], 0.0), vs[ci])], axis=1)
         for ci in chunks]
    reach = 1
    while True:
        x = [x[ci] + _bdot(a_pow[ci], x[ci]) for ci in chunks]
        reach *= 2
        if reach >= c:
            break
        a_pow = [_bdot(a_pow[ci], a_pow[ci]) for ci in chunks]
    z = [_bdot(jnp.where(incl, m[ci][n:, :n], 0.0), x[ci]) for ci in chunks]
    r_eff = [rs[ci] + z[ci][:, :LANES] for ci in chunks]
    y_loc = [z[ci][:, LANES:] + _bdot(jnp.where(incl, m[ci][n:, n:], 0.0), vs[ci]) for ci in chunks]
    g = [_bdot(x[ci][:, :LANES].T, bk[ci][:n]) for ci in chunks]
    h = [_bdot(jnp.concatenate([x[ci][:, LANES:], vs[ci]], axis=0).T, bk[ci]) for ci in chunks]

    state = state_sc[...]
    for ci in chunks:
        y = _bdot_nt(r_eff[ci], state) + y_loc[ci]
        y_ref[0, ci * c:(ci + 1) * c, :] = y[:c] + y[c:]
        state = (state + _bdot(state, g[ci]) + h[ci]) * w_end[ci]
    state_sc[...] = state


def _rwkv_scan(r, lw, k, v, a, b):
    bsz, t, _ = r.shape
    tc = min(SCAN_CHUNK * SCAN_CHUNKS_PER_STEP, t)
    spec = pl.BlockSpec((1, tc, LANES), lambda bi, p, c: (bi, c, p))
    return pl.pallas_call(
        _rwkv_scan_kernel,
        grid=(bsz, N_PAIRS, t // tc),
        in_specs=[spec] * 6,
        out_specs=spec,
        out_shape=jax.ShapeDtypeStruct((bsz, t, MIX_WIDTH), F32),
        scratch_shapes=[pltpu.VMEM((LANES, LANES), F32)],
        compiler_params=_params("parallel", "parallel", "arbitrary"),
        name="rwkv_scan",
    )(r, lw, k, v, a, b)


def _tail_kernel(*refs, rwkv, alpha):
    if rwkv:
        y_ref, bonus_ref, gate_ref, gng_ref, gnb_ref, seg_ref = refs[:6]
        refs = refs[6:]
    else:
        y_ref = refs[0]
        refs = refs[1:]
    (qm_ref, mk_ref, mv_ref, x_ref, wo_mix_ref, wo_mem_ref, ln1g_ref, ln1b_ref,
     w1_ref, w2_ref, ln2g_ref, ln2b_ref, o_ref) = refs

    if rwkv:
        y = y_ref[0]
        seg = seg_ref[...]
        mean = _split_dot(y, seg) * (1.0 / HEAD_DIM)
        d = y - mean
        var = _split_dot(d * d, seg) * (1.0 / HEAD_DIM)
        yn = d * lax.rsqrt(var + RWKV_GN_EPS) * gng_ref[...] + gnb_ref[...]
        o_mix = ((yn + bonus_ref[0]) * gate_ref[0]).astype(BF16)
    else:
        o_mix = y_ref[0]

    m_tokens = mk_ref.shape[1] // MEM_HEADS
    s = lax.dot_general(qm_ref[0], mk_ref[0], (((1,), (1,)), ((), ())), preferred_element_type=F32)
    probs = []
    for h in range(MEM_HEADS):
        sh = s[:, h * m_tokens:(h + 1) * m_tokens]
        e = jnp.exp(sh - jnp.max(sh, axis=1, keepdims=True))
        probs.append((e / jnp.sum(e, axis=1, keepdims=True)).astype(BF16))
    o_mem = jnp.dot(jnp.concatenate(probs, axis=1), mv_ref[0], preferred_element_type=F32)

    mix = (jnp.dot(o_mix, wo_mix_ref[...], preferred_element_type=F32)
           + _bdot(o_mem, wo_mem_ref[...]))
    x1 = _layer_norm(alpha * x_ref[0] + mix, ln1g_ref[...], ln1b_ref[...])
    x1b = x1.astype(BF16)
    ff = jnp.zeros(x1.shape, F32)
    for c in range(w1_ref.shape[1] // FF_CHUNK):
        cols = slice(c * FF_CHUNK, (c + 1) * FF_CHUNK)
        hidden = jnp.dot(x1b, w1_ref[:, cols], preferred_element_type=F32)
        hidden = jnp.square(jnp.maximum(hidden, 0.0)).astype(BF16)
        ff = ff + jnp.dot(hidden, w2_ref[cols, :], preferred_element_type=F32)
    o_ref[0] = _layer_norm(alpha * x1 + ff, ln2g_ref[...], ln2b_ref[...])


def _tail(mix_inputs, qm, mem_k, mem_v, x, wo_mix, wo_mem, ln1g, ln1b, w1, w2, ln2g, ln2b, *, rwkv, alpha):
    bsz, t, d = x.shape
    tm = min(ROW_TILE, t)
    row = lambda width: pl.BlockSpec((1, tm, width), lambda b, i: (b, i, 0))
    mem_spec = pl.BlockSpec((1,) + mem_k.shape[1:], lambda b, i: (b, 0, 0))
    single = lambda a: pl.BlockSpec(a.shape, lambda b, i: (0,) * a.ndim, pipeline_mode=pl.Buffered(1))
    if rwkv:
        y, bonus, gate, gng, gnb, seg = mix_inputs
        head_args = [y, bonus, gate, gng, gnb, seg]
        head_specs = [row(MIX_WIDTH)] * 3 + [single(gng), single(gnb), single(seg)]
    else:
        head_args = list(mix_inputs)
        head_specs = [row(MIX_WIDTH)]
    consts = (wo_mix, wo_mem, ln1g, ln1b, w1, w2, ln2g, ln2b)
    return pl.pallas_call(
        functools.partial(_tail_kernel, rwkv=rwkv, alpha=alpha),
        grid=(bsz, t // tm),
        in_specs=head_specs + [row(MEM_WIDTH), mem_spec, mem_spec, row(d)] + [single(c) for c in consts],
        out_specs=row(d),
        out_shape=jax.ShapeDtypeStruct((bsz, t, d), F32),
        compiler_params=_params("parallel", "parallel"),
        name="rwkv_tail" if rwkv else "mla_tail",
    )(*head_args, qm, mem_k, mem_v, x, *consts)


def _swap_halves(w):
    half = w.shape[-1] // 2
    return jnp.concatenate([w[..., half:], w[..., :half]], axis=-1)


def _pair_pack(w):
    kdim = w.shape[0]
    w = w.reshape(kdim, N_PAIRS, HEADS_PER_TILE * MLA_ROPE_DIM)
    pad = jnp.zeros((kdim, N_PAIRS, LANES - HEADS_PER_TILE * MLA_ROPE_DIM), w.dtype)
    return jnp.concatenate([w, pad], axis=-1).reshape(kdim, N_PAIRS * LANES)


def _mla_weights(w_in, w_q_up, w_kv_up):
    d = w_in.shape[0]
    o_kv, o_kr, o_qm = MLA_Q_RANK, MLA_Q_RANK + MLA_KV_RANK, MLA_Q_RANK + MLA_KV_RANK + MLA_ROPE_DIM
    w_kr = w_in[:, o_kr:o_qm]
    pad = jnp.zeros((d, LANES - 2 * MLA_ROPE_DIM), w_in.dtype)
    w_in_p = jnp.concatenate([w_in[:, :o_kr], w_in[:, o_qm:], w_kr, w_kr, pad,
                              _swap_halves(w_kr), _swap_halves(w_kr), pad], axis=1)
    wq = w_q_up.reshape(MLA_Q_RANK, MIX_HEADS, MLA_QK_DIM)
    rope = wq[:, :, MLA_NOPE_DIM:]
    w_q = jnp.concatenate([wq[:, :, :MLA_NOPE_DIM].reshape(MLA_Q_RANK, MIX_WIDTH),
                           _pair_pack(rope), _pair_pack(_swap_halves(rope))], axis=1)
    wkv = w_kv_up.reshape(MLA_KV_RANK, MIX_HEADS, 2 * HEAD_DIM)
    w_kv = jnp.concatenate([wkv[:, :, :HEAD_DIM].reshape(MLA_KV_RANK, MIX_WIDTH),
                            wkv[:, :, HEAD_DIM:].reshape(MLA_KV_RANK, MIX_WIDTH)], axis=1)
    return w_in_p.astype(BF16), w_q.astype(BF16), w_kv.astype(BF16)


def _rope_tables(positions):
    inv_freq = ROPE_BASE ** (-jnp.arange(ROPE_HALF, dtype=F32) * 2.0 / MLA_ROPE_DIM)
    ang = positions.astype(F32).reshape(-1, 1) * inv_freq
    cos, sin = jnp.cos(ang), jnp.sin(ang)
    pad = jnp.zeros((ang.shape[0], LANES - 2 * MLA_ROPE_DIM), F32)
    return (jnp.concatenate([cos, cos, cos, cos, pad], axis=1),
            jnp.concatenate([-sin, sin, -sin, sin, pad], axis=1))


def _pad_cols(w, width):
    return jnp.concatenate([w, jnp.zeros(w.shape[:-1] + (width - w.shape[-1],), w.dtype)], axis=-1)


def _pad_rows(w, height):
    return jnp.concatenate([w, jnp.zeros((height - w.shape[0],) + w.shape[1:], w.dtype)], axis=0)


def _rwkv_weights(w_in, mu, w2, a2, g2):
    o_hw, o_ha = 3 * MIX_WIDTH, 3 * MIX_WIDTH + RWKV_DECAY_RANK
    o_hg = o_ha + RWKV_A_RANK
    o_qm = o_hg + RWKV_GATE_RANK

    def layout(w):
        return jnp.concatenate([w[..., :o_hw], _pad_cols(w[..., o_hw:o_ha], LANES),
                                _pad_cols(w[..., o_ha:o_hg], LANES),
                                _pad_cols(w[..., o_hg:o_qm], 2 * LANES)], axis=-1)

    w_in_p = jnp.concatenate([layout(w_in[:, :o_qm]), w_in[:, o_qm:]], axis=1)
    return (w_in_p.astype(BF16), layout(mu).reshape(1, RW_TM_COLS),
            _pad_rows(w2, LANES).astype(BF16), _pad_rows(a2, LANES).astype(BF16),
            _pad_rows(g2, 2 * LANES).astype(BF16))


def kernel(x, mem, positions, mem_ln_g, mem_ln_b, w_mem_kv, mla_w_in, mla_q_norm, mla_w_q_up, mla_kv_norm, mla_w_kv_up, rwkv_w_in, rwkv_mu, rwkv_w0, rwkv_w2, rwkv_a0, rwkv_a2, rwkv_g2, rwkv_k_k, rwkv_k_a, rwkv_r_k, rwkv_gn_g, rwkv_gn_b, w_out, ln1_g, ln1_b, w_ff1, w_ff2, ln2_g, ln2_b):
    bsz, t, d = x.shape
    depth = w_out.shape[0]
    alpha = (2.0 * depth) ** 0.25
    row = lambda p: p.reshape(1, -1)

    mem_k, mem_v = _mem_kv(mem, mem_ln_g, mem_ln_b, w_mem_kv.astype(BF16))
    cos, sin = _rope_tables(positions)
    head_of = jnp.arange(MIX_WIDTH) // HEAD_DIM
    seg = (head_of[:, None] == head_of[None, :]).astype(BF16)

    for i in range(depth):
        j = i // 2
        tail_w = (w_out[i, :MIX_WIDTH].astype(BF16), w_out[i, MIX_WIDTH:].astype(BF16),
                  row(ln1_g[i]), row(ln1_b[i]), w_ff1[i].astype(BF16), w_ff2[i].astype(BF16),
                  row(ln2_g[i]), row(ln2_b[i]))
        if i % 2 == 0:
            w_in, w_q, w_kv = _mla_weights(mla_w_in[j], mla_w_q_up[j], mla_w_kv_up[j])
            qn, qr, kn, kr, v, qm = _mla_proj(x.reshape(bsz * t, d), cos, sin, w_in, row(mla_q_norm[j]), w_q,
                                              row(mla_kv_norm[j]), w_kv)
            seq = lambda a: a.reshape(bsz, t, a.shape[-1])
            o_mix = _mla_attn(seq(qn), seq(qr), seq(kn), seq(kr), seq(v))
            x = _tail((o_mix,), seq(qm), mem_k, mem_v, x, *tail_w, rwkv=False, alpha=alpha)
        else:
            w_in, mu, w2, a2, g2 = _rwkv_weights(rwkv_w_in[j], rwkv_mu[j], rwkv_w2[j], rwkv_a2[j], rwkv_g2[j])
            r, lw, k, v, a, b, gate, bonus, qm = _rwkv_proj(
                x, w_in, mu, row(rwkv_w0[j]), w2, row(rwkv_a0[j]), a2, g2, row(rwkv_k_k[j]), row(rwkv_k_a[j]),
                row(rwkv_r_k[j]), seg)
            y = _rwkv_scan(r, lw, k, v, a, b)
            x = _tail((y, bonus, gate, row(rwkv_gn_g[j]), row(rwkv_gn_b[j]), seg), qm, mem_k, mem_v, x,
                      *tail_w, rwkv=True, alpha=alpha)
    return x
```

```python
import functools

import jax
import jax.numpy as jnp
from jax import lax
from jax.experimental import pallas as pl
from jax.experimental.pallas import tpu as pltpu

F32 = jnp.float32
BF16 = jnp.bfloat16

HEAD_DIM = 64
MIX_HEADS = 12
MEM_HEADS = 4
MIX_WIDTH = MIX_HEADS * HEAD_DIM
MEM_WIDTH = MEM_HEADS * HEAD_DIM
MLA_Q_RANK = 384
MLA_KV_RANK = 256
MLA_NOPE_DIM = 64
MLA_ROPE_DIM = 32
MLA_QK_DIM = MLA_NOPE_DIM + MLA_ROPE_DIM
ROPE_HALF = MLA_ROPE_DIM // 2
ROPE_BASE = 10000.0
RWKV_DECAY_RANK = 64
RWKV_A_RANK = 64
RWKV_GATE_RANK = 160
RWKV_GN_EPS = 64e-5
LN_EPS = 1e-5
RMS_EPS = 1e-6
QK_SCALE = MLA_QK_DIM ** -0.5 * 1.4426950408889634
MEM_SCALE = HEAD_DIM ** -0.5
MASK_VALUE = -1e30

LANES = 128
MXU_DIM = 256
HEADS_PER_TILE = LANES // HEAD_DIM
N_PAIRS = MIX_HEADS // HEADS_PER_TILE
VMEM_LIMIT_BYTES = 56 * 1024 * 1024

ROW_TILE = 512
ATTN_TILE = 512
SCAN_CHUNK = 64
SCAN_CHUNKS_PER_STEP = 8
TAIL_ROW_GROUPS = 2

RW_HW = 3 * MIX_WIDTH
RW_HA = RW_HW + LANES
RW_HG = RW_HA + LANES
RW_TM_COLS = RW_HG + 2 * LANES
RW_COLS = RW_TM_COLS + MEM_WIDTH

ML_CKV = MLA_Q_RANK
ML_QMEM = ML_CKV + MLA_KV_RANK
ML_KR = ML_QMEM + MEM_WIDTH
ML_KR_SWAP = ML_KR + LANES
ML_COLS = ML_KR_SWAP + LANES


def _bdot(a, b):
    return jnp.dot(a.astype(BF16), b.astype(BF16), preferred_element_type=F32)


def _bdot_nt(a, b):
    return lax.dot_general(a.astype(BF16), b.astype(BF16), (((1,), (1,)), ((), ())),
                           preferred_element_type=F32)


def _head_sums(x, seg):
    hi = x.astype(BF16)
    lo = (x - hi.astype(F32)).astype(BF16)
    parts = []
    for g in range(x.shape[1] // MXU_DIM):
        cols = slice(g * MXU_DIM, (g + 1) * MXU_DIM)
        parts.append(jnp.dot(hi[:, cols], seg, preferred_element_type=F32)
                     + jnp.dot(lo[:, cols], seg, preferred_element_type=F32))
    return jnp.concatenate(parts, axis=1)


def _layer_norm(t, g, b):
    mu = jnp.mean(t, axis=-1, keepdims=True)
    d = t - mu
    var = jnp.mean(d * d, axis=-1, keepdims=True)
    return d * lax.rsqrt(var + LN_EPS) * g + b


def _rms_norm(t, g):
    return t * lax.rsqrt(jnp.mean(t * t, axis=-1, keepdims=True) + RMS_EPS) * g


def _sigmoid(z):
    return 1.0 / (1.0 + jnp.exp(-z))


def _params(*semantics):
    return pltpu.CompilerParams(dimension_semantics=semantics, vmem_limit_bytes=VMEM_LIMIT_BYTES)


def _const_spec(shape):
    nd = len(shape)
    return pl.BlockSpec(shape, lambda *_: (0,) * nd)


def _mem_kv_kernel(mem_ref, g_ref, b_ref, w_ref, k_ref, v_ref):
    m_tokens = mem_ref.shape[1]
    kv = _bdot(_layer_norm(mem_ref[0], g_ref[...], b_ref[...]), w_ref[...])
    k = kv[:, :MEM_WIDTH]
    v = kv[:, MEM_WIDTH:]
    head_of_lane = jnp.right_shift(lax.broadcasted_iota(jnp.int32, k.shape, 1), HEAD_DIM.bit_length() - 1)
    for h in range(MEM_HEADS):
        rows = slice(h * m_tokens, (h + 1) * m_tokens)
        k_ref[0, rows, :] = jnp.where(head_of_lane == h, k, 0.0).astype(BF16)
        v_ref[0, rows, :] = jnp.where(head_of_lane == h, v, 0.0).astype(BF16)


def _mem_kv(mem, g, b, w):
    bsz, m_tokens, d = mem.shape
    out = jax.ShapeDtypeStruct((bsz, MEM_HEADS * m_tokens, MEM_WIDTH), BF16)
    blk = pl.BlockSpec((1, MEM_HEADS * m_tokens, MEM_WIDTH), lambda i: (i, 0, 0))
    return pl.pallas_call(
        _mem_kv_kernel,
        grid=(bsz,),
        in_specs=[pl.BlockSpec((1, m_tokens, d), lambda i: (i, 0, 0)),
                  _const_spec((1, d)), _const_spec((1, d)), _const_spec(w.shape)],
        out_specs=[blk, blk],
        out_shape=[out, out],
        compiler_params=_params("parallel"),
        name="mem_kv",
    )(mem, g.reshape(1, d), b.reshape(1, d), w)


def _mla_proj_kernel(x_ref, cos_ref, sin_ref, w_in_ref, qg_ref, wq_ref, kvg_ref, wkv_ref,
                     qn_ref, qr_ref, kn_ref, kr_ref, v_ref, qm_ref):
    c = _bdot(x_ref[...], w_in_ref[...])
    q = _bdot(_rms_norm(c[:, :ML_CKV], qg_ref[...]), wq_ref[...])
    kv = _bdot(_rms_norm(c[:, ML_CKV:ML_QMEM], kvg_ref[...]), wkv_ref[...])
    cos = cos_ref[...]
    sin = sin_ref[...]
    cos_all = jnp.concatenate([cos] * N_PAIRS, axis=1)
    sin_all = jnp.concatenate([sin] * N_PAIRS, axis=1)
    qn_ref[...] = (q[:, :MIX_WIDTH] * QK_SCALE).astype(BF16)
    q_rot = q[:, MIX_WIDTH:2 * MIX_WIDTH] * cos_all + q[:, 2 * MIX_WIDTH:] * sin_all
    qr_ref[...] = (q_rot * QK_SCALE).astype(BF16)
    kn_ref[...] = kv[:, :MIX_WIDTH].astype(BF16)
    v_ref[...] = kv[:, MIX_WIDTH:].astype(BF16)
    kr_ref[...] = (c[:, ML_KR:ML_KR_SWAP] * cos + c[:, ML_KR_SWAP:] * sin).astype(BF16)
    qm_ref[...] = (c[:, ML_QMEM:ML_KR] * MEM_SCALE).astype(BF16)


def _mla_proj(x2d, cos, sin, w_in, q_norm, w_q, kv_norm, w_kv):
    n, d = x2d.shape
    tm = min(ROW_TILE, n)
    row = lambda width: pl.BlockSpec((tm, width), lambda i: (i, 0))
    shape = lambda width: jax.ShapeDtypeStruct((n, width), BF16)
    return pl.pallas_call(
        _mla_proj_kernel,
        grid=(n // tm,),
        in_specs=[row(d), row(LANES), row(LANES), _const_spec(w_in.shape), _const_spec(q_norm.shape),
                  _const_spec(w_q.shape), _const_spec(kv_norm.shape), _const_spec(w_kv.shape)],
        out_specs=[row(MIX_WIDTH), row(MIX_WIDTH), row(MIX_WIDTH), row(LANES), row(MIX_WIDTH), row(MEM_WIDTH)],
        out_shape=[shape(MIX_WIDTH), shape(MIX_WIDTH), shape(MIX_WIDTH), shape(LANES), shape(MIX_WIDTH),
                   shape(MEM_WIDTH)],
        compiler_params=_params("parallel"),
        name="mla_proj",
    )(x2d, cos, sin, w_in, q_norm, w_q, kv_norm, w_kv)


def _mla_attn_kernel(qn_ref, qr_ref, kn_ref, kr_ref, v_ref, o_ref, q_sc, m_sc, acc_sc):
    tq = qn_ref.shape[1]
    qi = pl.program_id(2)
    lane = lax.broadcasted_iota(jnp.int32, (tq, LANES), 1)
    qn = qn_ref[0]
    qr = qr_ref[0]
    zero = jnp.zeros_like(qn)
    q_sc[:tq, :LANES] = jnp.where(lane < HEAD_DIM, qn, zero)
    q_sc[:tq, LANES:] = jnp.where(lane < MLA_ROPE_DIM, qr, zero)
    q_sc[tq:, :LANES] = jnp.where(lane >= HEAD_DIM, qn, zero)
    q_sc[tq:, LANES:] = jnp.where(lane >= MLA_ROPE_DIM, qr, zero)
    m_sc[...] = jnp.full(m_sc.shape, -jnp.inf, F32)
    acc_sc[...] = jnp.zeros(acc_sc.shape, F32)

    ones = jnp.ones((tq, LANES), BF16)

    def step(j, masked):
        rows = pl.ds(pl.multiple_of(j * tq, tq), tq)
        k = jnp.concatenate([kn_ref[0, rows, :], kr_ref[0, rows, :]], axis=1)
        v_ones = jnp.concatenate([v_ref[0, rows, :], ones], axis=1)
        for h in range(HEADS_PER_TILE):
            hrows = slice(h * tq, (h + 1) * tq)
            s = lax.dot_general(q_sc[hrows, :], k, (((1,), (1,)), ((), ())), preferred_element_type=F32)
            if masked:
                q_pos = lax.broadcasted_iota(jnp.int32, s.shape, 0)
                k_pos = lax.broadcasted_iota(jnp.int32, s.shape, 1)
                s = jnp.where(k_pos <= q_pos, s, MASK_VALUE)
            m_prev = m_sc[hrows, :]
            m_new = jnp.maximum(m_prev, jnp.max(s, axis=1, keepdims=True))
            alpha = jnp.exp2(m_prev - m_new)
            p = jnp.exp2(s - jnp.concatenate([m_new] * (tq // LANES), axis=1))
            acc_sc[hrows, :] = (jnp.concatenate([alpha, alpha], axis=1) * acc_sc[hrows, :]
                                + jnp.dot(p.astype(BF16), v_ones, preferred_element_type=F32))
            m_sc[hrows, :] = m_new

    def body(j, carry):
        step(j, False)
        return carry

    lax.fori_loop(0, qi, body, 0)
    step(qi, True)
    out = acc_sc[:, :LANES] / acc_sc[:, LANES:]
    o_ref[0] = jnp.where(lane < HEAD_DIM, out[:tq], out[tq:]).astype(BF16)


def _mla_attn(qn, qr, kn, kr, v):
    bsz, t, _ = qn.shape
    tq = min(ATTN_TILE, t)
    assert tq & (tq - 1) == 0 and t % tq == 0
    q_spec = pl.BlockSpec((1, tq, LANES), lambda b, p, i: (b, i, p))
    seq_spec = pl.BlockSpec((1, t, LANES), lambda b, p, i: (b, 0, p))
    return pl.pallas_call(
        _mla_attn_kernel,
        grid=(bsz, N_PAIRS, t // tq),
        in_specs=[q_spec, q_spec, seq_spec, pl.BlockSpec((1, t, LANES), lambda b, p, i: (b, 0, 0)), seq_spec],
        out_specs=q_spec,
        out_shape=jax.ShapeDtypeStruct((bsz, t, MIX_WIDTH), BF16),
        scratch_shapes=[pltpu.VMEM((HEADS_PER_TILE * tq, 2 * LANES), BF16),
                        pltpu.VMEM((HEADS_PER_TILE * tq, LANES), F32),
                        pltpu.VMEM((HEADS_PER_TILE * tq, 2 * LANES), F32)],
        compiler_params=_params("parallel", "parallel", "arbitrary"),
        name="mla_attn",
    )(qn, qr, kn, kr, v)


def _rwkv_proj_kernel(x_ref, w_in_ref, mu_ref, w0_ref, w2_ref, a0_ref, a2_ref, g2_ref, kk_ref, ka_ref,
                      rk_ref, seg_ref, r_o, lw_o, k_o, v_o, a_o, b_o, g_o, bonus_o, qm_o, carry_sc):
    tm = x_ref.shape[1]

    @pl.when(pl.program_id(1) == 0)
    def _():
        carry_sc[...] = jnp.zeros(carry_sc.shape, F32)

    proj = _bdot(x_ref[0], w_in_ref[...])
    prev = carry_sc[7:8, :]
    carry_sc[...] = proj[tm - 8:, :RW_TM_COLS]
    first_row = lax.broadcasted_iota(jnp.int32, (tm, 1), 0) == 0

    def mixed(lo, hi):
        cur = proj[:, lo:hi]
        shifted = jnp.where(first_row, prev[:, lo:hi], pltpu.roll(cur, 1, 0))
        return cur + (shifted - cur) * mu_ref[:, lo:hi]

    r = mixed(0, MIX_WIDTH)
    k = mixed(MIX_WIDTH, 2 * MIX_WIDTH)
    v = mixed(2 * MIX_WIDTH, RW_HW)
    hw = mixed(RW_HW, RW_HA)
    ha = mixed(RW_HA, RW_HG)
    hg = mixed(RW_HG, RW_TM_COLS)

    z = -(w0_ref[...] + _bdot(jnp.tanh(hw), w2_ref[...]))
    softplus = jnp.maximum(z, 0.0) + jnp.log(1.0 + jnp.exp(-jnp.abs(z)))
    log_decay = -jnp.exp(-softplus - 0.5)
    a = _sigmoid(a0_ref[...] + _bdot(ha, a2_ref[...]))
    gate = _bdot(_sigmoid(hg), g2_ref[...])

    seg = seg_ref[...]
    kk = k * kk_ref[...]
    kk = kk / jnp.maximum(jnp.sqrt(_head_sums(kk * kk, seg)), 1e-12)
    k2 = k * (1.0 + (a - 1.0) * ka_ref[...])
    bonus = _head_sums(r * k2 * rk_ref[...], seg) * v

    r_o[0] = r
    lw_o[0] = log_decay
    k_o[0] = k2
    v_o[0] = v
    a_o[0] = -kk
    b_o[0] = kk * a
    g_o[0] = gate
    bonus_o[0] = bonus
    qm_o[0] = (proj[:, RW_TM_COLS:] * MEM_SCALE).astype(BF16)


def _rwkv_proj(x, w_in, mu, w0, w2, a0, a2, g2, k_k, k_a, r_k, seg):
    bsz, t, d = x.shape
    tm = min(ROW_TILE // 2, t)
    row = lambda width: pl.BlockSpec((1, tm, width), lambda b, i: (b, i, 0))
    wide = jax.ShapeDtypeStruct((bsz, t, MIX_WIDTH), F32)
    consts = (w_in, mu, w0, w2, a0, a2, g2, k_k, k_a, r_k, seg)
    return pl.pallas_call(
        _rwkv_proj_kernel,
        grid=(bsz, t // tm),
        in_specs=[row(d)] + [_const_spec(c.shape) for c in consts],
        out_specs=[row(MIX_WIDTH)] * 8 + [row(MEM_WIDTH)],
        out_shape=[wide] * 8 + [jax.ShapeDtypeStruct((bsz, t, MEM_WIDTH), BF16)],
        scratch_shapes=[pltpu.VMEM((8, RW_TM_COLS), F32)],
        compiler_params=_params("parallel", "arbitrary"),
        name="rwkv_proj",
    )(x, *consts)


def _rwkv_scan_kernel(r_ref, lw_ref, k_ref, v_ref, a_ref, b_ref, y_ref, state_sc):
    c = SCAN_CHUNK
    n = HEADS_PER_TILE * c
    chunks = range(r_ref.shape[1] // c)

    @pl.when(pl.program_id(2) == 0)
    def _():
        state_sc[...] = jnp.zeros(state_sc.shape, F32)

    head0 = lax.broadcasted_iota(jnp.int32, (c, LANES), 1) < HEAD_DIM

    def stack(z):
        return jnp.concatenate([jnp.where(head0, z, 0.0), jnp.where(head0, 0.0, z)], axis=0)

    tril_ones = jnp.where(lax.broadcasted_iota(jnp.int32, (c, c), 0)
                          >= lax.broadcasted_iota(jnp.int32, (c, c), 1), 1.0, 0.0).astype(BF16)
    i2 = lax.broadcasted_iota(jnp.int32, (n, n), 0)
    j2 = lax.broadcasted_iota(jnp.int32, (n, n), 1)
    strict = i2 > j2
    incl = i2 >= j2

    rs, as_, bk, vs, w_end = [], [], [], [], []
    for ci in chunks:
        rows = slice(ci * c, (ci + 1) * c)
        lw = lw_ref[0, rows, :]
        lw_hi = lw.astype(BF16)
        lw_mid = (lw - lw_hi.astype(F32)).astype(BF16)
        lw_lo = (lw - lw_hi.astype(F32) - lw_mid.astype(F32)).astype(BF16)
        cum = (jnp.dot(tril_ones, lw_hi, preferred_element_type=F32)
               + jnp.dot(tril_ones, lw_mid, preferred_element_type=F32)
               + jnp.dot(tril_ones, lw_lo, preferred_element_type=F32))
        w_incl = jnp.exp(cum)
        w_inv = jnp.exp(-cum)
        rs.append(stack(r_ref[0, rows, :] * w_incl))
        as_.append(stack(a_ref[0, rows, :] * jnp.exp(cum - lw)))
        bk.append(jnp.concatenate([stack(b_ref[0, rows, :] * w_inv), stack(k_ref[0, rows, :] * w_inv)],
                                  axis=0).astype(BF16))
        vs.append(stack(v_ref[0, rows, :]))
        w_end.append(w_incl[c - 1:c, :])

    m = [_bdot_nt(jnp.concatenate([as_[ci], rs[ci]], axis=0), bk[ci]) for ci in chunks]
    a_pow = [jnp.where(strict, m[ci][:n, :n], 0.0) for ci in chunks]
    inv_m1 = a_pow
    reach = 2
    while reach < c:
        a_pow = [_bdot(a_pow[ci], a_pow[ci]) for ci in chunks]
        inv_m1 = [inv_m1[ci] + a_pow[ci] + _bdot(inv_m1[ci], a_pow[ci]) for ci in chunks]
        reach *= 2
    x = [jnp.concatenate([as_[ci], _bdot(jnp.where(strict, m[ci][:n, n:], 0.0), vs[ci])], axis=1)
         for ci in chunks]
    x = [x[ci] + _bdot(inv_m1[ci], x[ci]) for ci in chunks]
    z = [_bdot(jnp.where(incl, m[ci][n:, :n], 0.0), x[ci]) for ci in chunks]
    r_eff = [rs[ci] + z[ci][:, :LANES] for ci in chunks]
    y_loc = [z[ci][:, LANES:] + _bdot(jnp.where(incl, m[ci][n:, n:], 0.0), vs[ci]) for ci in chunks]
    g = [_bdot(x[ci][:, :LANES].T, bk[ci][:n]) for ci in chunks]
    h = [_bdot(jnp.concatenate([x[ci][:, LANES:], vs[ci]], axis=0).T, bk[ci]) for ci in chunks]

    state = state_sc[...]
    for ci in chunks:
        y = _bdot_nt(r_eff[ci], state) + y_loc[ci]
        y_ref[0, ci * c:(ci + 1) * c, :] = y[:c] + y[c:]
        state = (state + _bdot(state, g[ci]) + h[ci]) * w_end[ci]
    state_sc[...] = state


def _rwkv_scan(r, lw, k, v, a, b):
    bsz, t, _ = r.shape
    tc = min(SCAN_CHUNK * SCAN_CHUNKS_PER_STEP, t)
    spec = pl.BlockSpec((1, tc, LANES), lambda bi, p, c: (bi, c, p))
    return pl.pallas_call(
        _rwkv_scan_kernel,
        grid=(bsz, N_PAIRS, t // tc),
        in_specs=[spec] * 6,
        out_specs=spec,
        out_shape=jax.ShapeDtypeStruct((bsz, t, MIX_WIDTH), F32),
        scratch_shapes=[pltpu.VMEM((LANES, LANES), F32)],
        compiler_params=_params("parallel", "parallel", "arbitrary"),
        name="rwkv_scan",
    )(r, lw, k, v, a, b)


def _tail_kernel(*refs, rwkv, alpha):
    if rwkv:
        y_ref, bonus_ref, gate_ref, gng_ref, gnb_ref, seg_ref = refs[:6]
        refs = refs[6:]
    else:
        y_ref = refs[0]
        refs = refs[1:]
    (qm_ref, mk_ref, mv_ref, x_ref, wo_mix_ref, wo_mem_ref, ln1g_ref, ln1b_ref,
     w1_ref, w2_ref, ln2g_ref, ln2b_ref, o_ref) = refs

    m_tokens = mk_ref.shape[1] // MEM_HEADS
    tm = x_ref.shape[1]
    groups = [slice(i * (tm // TAIL_ROW_GROUPS), (i + 1) * (tm // TAIL_ROW_GROUPS))
              for i in range(TAIL_ROW_GROUPS)]

    def mixed_and_normed(rows):
        if rwkv:
            y = y_ref[0, rows, :]
            seg = seg_ref[...]
            mean = _head_sums(y, seg) * (1.0 / HEAD_DIM)
            d = y - mean
            var = _head_sums(d * d, seg) * (1.0 / HEAD_DIM)
            yn = d * lax.rsqrt(var + RWKV_GN_EPS) * gng_ref[...] + gnb_ref[...]
            o_mix = ((yn + bonus_ref[0, rows, :]) * gate_ref[0, rows, :]).astype(BF16)
        else:
            o_mix = y_ref[0, rows, :]
        s = lax.dot_general(qm_ref[0, rows, :], mk_ref[0], (((1,), (1,)), ((), ())),
                            preferred_element_type=F32)
        probs = []
        for h in range(MEM_HEADS):
            sh = s[:, h * m_tokens:(h + 1) * m_tokens]
            e = jnp.exp(sh - jnp.max(sh, axis=1, keepdims=True))
            probs.append((e / jnp.sum(e, axis=1, keepdims=True)).astype(BF16))
        o_mem = jnp.dot(jnp.concatenate(probs, axis=1), mv_ref[0], preferred_element_type=F32)
        mix = (jnp.dot(o_mix, wo_mix_ref[...], preferred_element_type=F32)
               + _bdot(o_mem, wo_mem_ref[...]))
        return _layer_norm(alpha * x_ref[0, rows, :] + mix, ln1g_ref[...], ln1b_ref[...])

    def mlp(x1):
        hidden = jnp.dot(x1.astype(BF16), w1_ref[...], preferred_element_type=F32)
        hidden = jnp.square(jnp.maximum(hidden, 0.0)).astype(BF16)
        ff = jnp.dot(hidden, w2_ref[...], preferred_element_type=F32)
        return _layer_norm(alpha * x1 + ff, ln2g_ref[...], ln2b_ref[...])

    x1 = [mixed_and_normed(rows) for rows in groups]
    for rows, x1_rows in zip(groups, x1):
        o_ref[0, rows, :] = mlp(x1_rows)


def _tail(mix_inputs, qm, mem_k, mem_v, x, wo_mix, wo_mem, ln1g, ln1b, w1, w2, ln2g, ln2b, *, rwkv, alpha):
    bsz, t, d = x.shape
    tm = min(ROW_TILE, t)
    row = lambda width: pl.BlockSpec((1, tm, width), lambda b, i: (b, i, 0))
    mem_spec = pl.BlockSpec((1,) + mem_k.shape[1:], lambda b, i: (b, 0, 0))
    single = lambda a: pl.BlockSpec(a.shape, lambda b, i: (0,) * a.ndim, pipeline_mode=pl.Buffered(1))
    if rwkv:
        y, bonus, gate, gng, gnb, seg = mix_inputs
        head_args = [y, bonus, gate, gng, gnb, seg]
        head_specs = [row(MIX_WIDTH)] * 3 + [single(gng), single(gnb), single(seg)]
    else:
        head_args = list(mix_inputs)
        head_specs = [row(MIX_WIDTH)]
    consts = (wo_mix, wo_mem, ln1g, ln1b, w1, w2, ln2g, ln2b)
    return pl.pallas_call(
        functools.partial(_tail_kernel, rwkv=rwkv, alpha=alpha),
        grid=(bsz, t // tm),
        in_specs=head_specs + [row(MEM_WIDTH), mem_spec, mem_spec, row(d)] + [single(c) for c in consts],
        out_specs=row(d),
        out_shape=jax.ShapeDtypeStruct((bsz, t, d), F32),
        compiler_params=_params("parallel", "parallel"),
        name="rwkv_tail" if rwkv else "mla_tail",
    )(*head_args, qm, mem_k, mem_v, x, *consts)


def _swap_halves(w):
    half = w.shape[-1] // 2
    return jnp.concatenate([w[..., half:], w[..., :half]], axis=-1)


def _pair_pack(w):
    kdim = w.shape[0]
    w = w.reshape(kdim, N_PAIRS, HEADS_PER_TILE * MLA_ROPE_DIM)
    pad = jnp.zeros((kdim, N_PAIRS, LANES - HEADS_PER_TILE * MLA_ROPE_DIM), w.dtype)
    return jnp.concatenate([w, pad], axis=-1).reshape(kdim, N_PAIRS * LANES)


def _mla_weights(w_in, w_q_up, w_kv_up):
    d = w_in.shape[0]
    o_kv, o_kr, o_qm = MLA_Q_RANK, MLA_Q_RANK + MLA_KV_RANK, MLA_Q_RANK + MLA_KV_RANK + MLA_ROPE_DIM
    w_kr = w_in[:, o_kr:o_qm]
    pad = jnp.zeros((d, LANES - 2 * MLA_ROPE_DIM), w_in.dtype)
    w_in_p = jnp.concatenate([w_in[:, :o_kr], w_in[:, o_qm:], w_kr, w_kr, pad,
                              _swap_halves(w_kr), _swap_halves(w_kr), pad], axis=1)
    wq = w_q_up.reshape(MLA_Q_RANK, MIX_HEADS, MLA_QK_DIM)
    rope = wq[:, :, MLA_NOPE_DIM:]
    w_q = jnp.concatenate([wq[:, :, :MLA_NOPE_DIM].reshape(MLA_Q_RANK, MIX_WIDTH),
                           _pair_pack(rope), _pair_pack(_swap_halves(rope))], axis=1)
    wkv = w_kv_up.reshape(MLA_KV_RANK, MIX_HEADS, 2 * HEAD_DIM)
    w_kv = jnp.concatenate([wkv[:, :, :HEAD_DIM].reshape(MLA_KV_RANK, MIX_WIDTH),
                            wkv[:, :, HEAD_DIM:].reshape(MLA_KV_RANK, MIX_WIDTH)], axis=1)
    return w_in_p.astype(BF16), w_q.astype(BF16), w_kv.astype(BF16)


def _rope_tables(positions):
    inv_freq = ROPE_BASE ** (-jnp.arange(ROPE_HALF, dtype=F32) * 2.0 / MLA_ROPE_DIM)
    ang = positions.astype(F32).reshape(-1, 1) * inv_freq
    cos, sin = jnp.cos(ang), jnp.sin(ang)
    pad = jnp.zeros((ang.shape[0], LANES - 2 * MLA_ROPE_DIM), F32)
    return (jnp.concatenate([cos, cos, cos, cos, pad], axis=1),
            jnp.concatenate([-sin, sin, -sin, sin, pad], axis=1))


def _pad_cols(w, width):
    return jnp.concatenate([w, jnp.zeros(w.shape[:-1] + (width - w.shape[-1],), w.dtype)], axis=-1)


def _pad_rows(w, height):
    return jnp.concatenate([w, jnp.zeros((height - w.shape[0],) + w.shape[1:], w.dtype)], axis=0)


def _rwkv_weights(w_in, mu, w2, a2, g2):
    o_hw, o_ha = 3 * MIX_WIDTH, 3 * MIX_WIDTH + RWKV_DECAY_RANK
    o_hg = o_ha + RWKV_A_RANK
    o_qm = o_hg + RWKV_GATE_RANK

    def layout(w):
        return jnp.concatenate([w[..., :o_hw], _pad_cols(w[..., o_hw:o_ha], LANES),
                                _pad_cols(w[..., o_ha:o_hg], LANES),
                                _pad_cols(w[..., o_hg:o_qm], 2 * LANES)], axis=-1)

    w_in_p = jnp.concatenate([layout(w_in[:, :o_qm]), w_in[:, o_qm:]], axis=1)
    return (w_in_p.astype(BF16), layout(mu).reshape(1, RW_TM_COLS),
            _pad_rows(w2, LANES).astype(BF16), _pad_rows(a2, LANES).astype(BF16),
            _pad_rows(g2, 2 * LANES).astype(BF16))


def kernel(x, mem, positions, mem_ln_g, mem_ln_b, w_mem_kv, mla_w_in, mla_q_norm, mla_w_q_up, mla_kv_norm, mla_w_kv_up, rwkv_w_in, rwkv_mu, rwkv_w0, rwkv_w2, rwkv_a0, rwkv_a2, rwkv_g2, rwkv_k_k, rwkv_k_a, rwkv_r_k, rwkv_gn_g, rwkv_gn_b, w_out, ln1_g, ln1_b, w_ff1, w_ff2, ln2_g, ln2_b):
    bsz, t, d = x.shape
    depth = w_out.shape[0]
    alpha = (2.0 * depth) ** 0.25
    row = lambda p: p.reshape(1, -1)

    mem_k, mem_v = _mem_kv(mem, mem_ln_g, mem_ln_b, w_mem_kv.astype(BF16))
    cos, sin = _rope_tables(positions)
    head_of = jnp.arange(MXU_DIM) // HEAD_DIM
    seg = (head_of[:, None] == head_of[None, :]).astype(BF16)

    for i in range(depth):
        j = i // 2
        tail_w = (w_out[i, :MIX_WIDTH].astype(BF16), w_out[i, MIX_WIDTH:].astype(BF16),
                  row(ln1_g[i]), row(ln1_b[i]), w_ff1[i].astype(BF16), w_ff2[i].astype(BF16),
                  row(ln2_g[i]), row(ln2_b[i]))
        if i % 2 == 0:
            w_in, w_q, w_kv = _mla_weights(mla_w_in[j], mla_w_q_up[j], mla_w_kv_up[j])
            qn, qr, kn, kr, v, qm = _mla_proj(x.reshape(bsz * t, d), cos, sin, w_in, row(mla_q_norm[j]), w_q,
                                              row(mla_kv_norm[j]), w_kv)
            seq = lambda a: a.reshape(bsz, t, a.shape[-1])
            o_mix = _mla_attn(seq(qn), seq(qr), seq(kn), seq(kr), seq(v))
            x = _tail((o_mix,), seq(qm), mem_k, mem_v, x, *tail_w, rwkv=False, alpha=alpha)
        else:
            w_in, mu, w2, a2, g2 = _rwkv_weights(rwkv_w_in[j], rwkv_mu[j], rwkv_w2[j], rwkv_a2[j], rwkv_g2[j])
            r, lw, k, v, a, b, gate, bonus, qm = _rwkv_proj(
                x, w_in, mu, row(rwkv_w0[j]), w2, row(rwkv_a0[j]), a2, g2, row(rwkv_k_k[j]), row(rwkv_k_a[j]),
                row(rwkv_r_k[j]), seg)
            y = _rwkv_scan(r, lw, k, v, a, b)
            x = _tail((y, bonus, gate, row(rwkv_gn_g[j]), row(rwkv_gn_b[j]), seg), qm, mem_k, mem_v, x,
                      *tail_w, rwkv=True, alpha=alpha)
    return x
```

```python
import functools

import jax
import jax.numpy as jnp
from jax import lax
from jax.experimental import pallas as pl
from jax.experimental.pallas import tpu as pltpu

F32 = jnp.float32
BF16 = jnp.bfloat16

HEAD_DIM = 64
MIX_HEADS = 12
MEM_HEADS = 4
MIX_WIDTH = MIX_HEADS * HEAD_DIM
MEM_WIDTH = MEM_HEADS * HEAD_DIM
MLA_Q_RANK = 384
MLA_KV_RANK = 256
MLA_NOPE_DIM = 64
MLA_ROPE_DIM = 32
MLA_QK_DIM = MLA_NOPE_DIM + MLA_ROPE_DIM
ROPE_HALF = MLA_ROPE_DIM // 2
ROPE_BASE = 10000.0
RWKV_DECAY_RANK = 64
RWKV_A_RANK = 64
RWKV_GATE_RANK = 160
RWKV_GN_EPS = 64e-5
LN_EPS = 1e-5
RMS_EPS = 1e-6
QK_SCALE = MLA_QK_DIM ** -0.5 * 1.4426950408889634
MEM_SCALE = HEAD_DIM ** -0.5
MASK_VALUE = -1e30

LANES = 128
MXU_DIM = 256
HEADS_PER_TILE = LANES // HEAD_DIM
N_PAIRS = MIX_HEADS // HEADS_PER_TILE
VMEM_LIMIT_BYTES = 56 * 1024 * 1024

ROW_TILE = 512
ATTN_TILE = 512
ATTN_ROW_BLOCKS = 4
SCAN_CHUNK = 64
SCAN_CHUNKS_PER_STEP = 16
ROW_GROUPS = 2

RW_HW = 3 * MIX_WIDTH
RW_HA = RW_HW + LANES
RW_HG = RW_HA + LANES
RW_TM_COLS = RW_HG + 2 * LANES
RW_COLS = RW_TM_COLS + MEM_WIDTH

ML_CKV = MLA_Q_RANK
ML_QMEM = ML_CKV + MLA_KV_RANK
ML_KR = ML_QMEM + MEM_WIDTH
ML_KR_SWAP = ML_KR + LANES
ML_COLS = ML_KR_SWAP + LANES


def _bdot(a, b):
    return jnp.dot(a.astype(BF16), b.astype(BF16), preferred_element_type=F32)


def _bdot_nt(a, b):
    return lax.dot_general(a.astype(BF16), b.astype(BF16), (((1,), (1,)), ((), ())),
                           preferred_element_type=F32)


def _head_sums(x, seg):
    hi = x.astype(BF16)
    lo = (x - hi.astype(F32)).astype(BF16)
    parts = []
    for g in range(x.shape[1] // MXU_DIM):
        cols = slice(g * MXU_DIM, (g + 1) * MXU_DIM)
        parts.append(jnp.dot(hi[:, cols], seg, preferred_element_type=F32)
                     + jnp.dot(lo[:, cols], seg, preferred_element_type=F32))
    return jnp.concatenate(parts, axis=1)


def _layer_norm(t, g, b):
    mu = jnp.mean(t, axis=-1, keepdims=True)
    d = t - mu
    var = jnp.mean(d * d, axis=-1, keepdims=True)
    return d * lax.rsqrt(var + LN_EPS) * g + b


def _rms_norm(t, g):
    return t * lax.rsqrt(jnp.mean(t * t, axis=-1, keepdims=True) + RMS_EPS) * g


def _sigmoid(z):
    return 1.0 / (1.0 + jnp.exp(-z))


def _params(*semantics):
    return pltpu.CompilerParams(dimension_semantics=semantics, vmem_limit_bytes=VMEM_LIMIT_BYTES)


def _const_spec(shape):
    nd = len(shape)
    return pl.BlockSpec(shape, lambda *_: (0,) * nd)


def _mem_kv_kernel(mem_ref, g_ref, b_ref, w_ref, k_ref, v_ref):
    m_tokens = mem_ref.shape[1]
    kv = _bdot(_layer_norm(mem_ref[0], g_ref[...], b_ref[...]), w_ref[...])
    k = kv[:, :MEM_WIDTH]
    v = kv[:, MEM_WIDTH:]
    head_of_lane = jnp.right_shift(lax.broadcasted_iota(jnp.int32, k.shape, 1), HEAD_DIM.bit_length() - 1)
    for h in range(MEM_HEADS):
        rows = slice(h * m_tokens, (h + 1) * m_tokens)
        k_ref[0, rows, :] = jnp.where(head_of_lane == h, k, 0.0).astype(BF16)
        v_ref[0, rows, :] = jnp.where(head_of_lane == h, v, 0.0).astype(BF16)


def _mem_kv(mem, g, b, w):
    bsz, m_tokens, d = mem.shape
    out = jax.ShapeDtypeStruct((bsz, MEM_HEADS * m_tokens, MEM_WIDTH), BF16)
    blk = pl.BlockSpec((1, MEM_HEADS * m_tokens, MEM_WIDTH), lambda i: (i, 0, 0))
    return pl.pallas_call(
        _mem_kv_kernel,
        grid=(bsz,),
        in_specs=[pl.BlockSpec((1, m_tokens, d), lambda i: (i, 0, 0)),
                  _const_spec((1, d)), _const_spec((1, d)), _const_spec(w.shape)],
        out_specs=[blk, blk],
        out_shape=[out, out],
        compiler_params=_params("parallel"),
        name="mem_kv",
    )(mem, g.reshape(1, d), b.reshape(1, d), w)


def _mla_proj_kernel(x_ref, cos_ref, sin_ref, w_in_ref, qg_ref, wq_ref, kvg_ref, wkv_ref,
                     qn_ref, qr_ref, kn_ref, kr_ref, v_ref, qm_ref):
    c = _bdot(x_ref[...], w_in_ref[...])
    q = _bdot(_rms_norm(c[:, :ML_CKV], qg_ref[...]), wq_ref[...])
    kv = _bdot(_rms_norm(c[:, ML_CKV:ML_QMEM], kvg_ref[...]), wkv_ref[...])
    cos = cos_ref[...]
    sin = sin_ref[...]
    cos_all = jnp.concatenate([cos] * N_PAIRS, axis=1)
    sin_all = jnp.concatenate([sin] * N_PAIRS, axis=1)
    qn_ref[...] = (q[:, :MIX_WIDTH] * QK_SCALE).astype(BF16)
    q_rot = q[:, MIX_WIDTH:2 * MIX_WIDTH] * cos_all + q[:, 2 * MIX_WIDTH:] * sin_all
    qr_ref[...] = (q_rot * QK_SCALE).astype(BF16)
    kn_ref[...] = kv[:, :MIX_WIDTH].astype(BF16)
    v_ref[...] = kv[:, MIX_WIDTH:].astype(BF16)
    kr_ref[...] = (c[:, ML_KR:ML_KR_SWAP] * cos + c[:, ML_KR_SWAP:] * sin).astype(BF16)
    qm_ref[...] = (c[:, ML_QMEM:ML_KR] * MEM_SCALE).astype(BF16)


def _mla_proj(x2d, cos, sin, w_in, q_norm, w_q, kv_norm, w_kv):
    n, d = x2d.shape
    tm = min(ROW_TILE, n)
    row = lambda width: pl.BlockSpec((tm, width), lambda i: (i, 0))
    shape = lambda width: jax.ShapeDtypeStruct((n, width), BF16)
    return pl.pallas_call(
        _mla_proj_kernel,
        grid=(n // tm,),
        in_specs=[row(d), row(LANES), row(LANES), _const_spec(w_in.shape), _const_spec(q_norm.shape),
                  _const_spec(w_q.shape), _const_spec(kv_norm.shape), _const_spec(w_kv.shape)],
        out_specs=[row(MIX_WIDTH), row(MIX_WIDTH), row(MIX_WIDTH), row(LANES), row(MIX_WIDTH), row(MEM_WIDTH)],
        out_shape=[shape(MIX_WIDTH), shape(MIX_WIDTH), shape(MIX_WIDTH), shape(LANES), shape(MIX_WIDTH),
                   shape(MEM_WIDTH)],
        compiler_params=_params("parallel"),
        name="mla_proj",
    )(x2d, cos, sin, w_in, q_norm, w_q, kv_norm, w_kv)


def _mla_attn_kernel(qn_ref, qr_ref, kn_ref, kr_ref, v_ref, o_ref, q_sc, m_sc, acc_sc, sa_sc, sb_sc):
    tq = qn_ref.shape[1]
    qi = pl.program_id(2)
    lane = lax.broadcasted_iota(jnp.int32, (tq, LANES), 1)
    qn = qn_ref[0]
    qr = qr_ref[0]
    zero = jnp.zeros_like(qn)
    q_sc[:tq, :LANES] = jnp.where(lane < HEAD_DIM, qn, zero)
    q_sc[:tq, LANES:] = jnp.where(lane < MLA_ROPE_DIM, qr, zero)
    q_sc[tq:, :LANES] = jnp.where(lane >= HEAD_DIM, qn, zero)
    q_sc[tq:, LANES:] = jnp.where(lane >= MLA_ROPE_DIM, qr, zero)
    m_sc[...] = jnp.full(m_sc.shape, -jnp.inf, F32)
    acc_sc[...] = jnp.zeros(acc_sc.shape, F32)

    ones = jnp.ones((tq, LANES), BF16)
    rb = HEADS_PER_TILE * tq // ATTN_ROW_BLOCKS

    def scores(j, s_ref):
        rows = pl.ds(pl.multiple_of(j * tq, tq), tq)
        k = jnp.concatenate([kn_ref[0, rows, :], kr_ref[0, rows, :]], axis=1)
        s_ref[...] = lax.dot_general(q_sc[...], k, (((1,), (1,)), ((), ())), preferred_element_type=F32)

    def consume(j, s_ref, masked):
        rows = pl.ds(pl.multiple_of(j * tq, tq), tq)
        v_ones = jnp.concatenate([v_ref[0, rows, :], ones], axis=1)
        for blk in range(ATTN_ROW_BLOCKS):
            brows = slice(blk * rb, (blk + 1) * rb)
            s = s_ref[brows, :]
            if masked:
                q_pos = lax.broadcasted_iota(jnp.int32, s.shape, 0) + (blk * rb) % tq
                k_pos = lax.broadcasted_iota(jnp.int32, s.shape, 1)
                s = jnp.where(k_pos <= q_pos, s, MASK_VALUE)
            m_prev = m_sc[brows, :]
            m_new = jnp.maximum(m_prev, jnp.max(s, axis=1, keepdims=True))
            alpha = jnp.exp2(m_prev - m_new)
            p = jnp.exp2(s - jnp.concatenate([m_new] * (tq // LANES), axis=1))
            acc_sc[brows, :] = (jnp.concatenate([alpha, alpha], axis=1) * acc_sc[brows, :]
                                + jnp.dot(p.astype(BF16), v_ones, preferred_element_type=F32))
            m_sc[brows, :] = m_new

    scores(0, sa_sc)

    def body(i, carry):
        scores(2 * i + 1, sb_sc)
        consume(2 * i, sa_sc, False)
        scores(2 * i + 2, sa_sc)
        consume(2 * i + 1, sb_sc, False)
        return carry

    lax.fori_loop(0, qi // 2, body, 0)

    @pl.when(qi % 2 == 0)
    def _():
        consume(qi, sa_sc, True)

    @pl.when(qi % 2 == 1)
    def _():
        scores(qi, sb_sc)
        consume(qi - 1, sa_sc, False)
        consume(qi, sb_sc, True)

    out = acc_sc[:, :LANES] / acc_sc[:, LANES:]
    o_ref[0] = jnp.where(lane < HEAD_DIM, out[:tq], out[tq:]).astype(BF16)


def _mla_attn(qn, qr, kn, kr, v):
    bsz, t, _ = qn.shape
    tq = min(ATTN_TILE, t)
    assert tq & (tq - 1) == 0 and t % tq == 0
    q_spec = pl.BlockSpec((1, tq, LANES), lambda b, p, i: (b, i, p))
    seq_spec = pl.BlockSpec((1, t, LANES), lambda b, p, i: (b, 0, p))
    return pl.pallas_call(
        _mla_attn_kernel,
        grid=(bsz, N_PAIRS, t // tq),
        in_specs=[q_spec, q_spec, seq_spec, pl.BlockSpec((1, t, LANES), lambda b, p, i: (b, 0, 0)), seq_spec],
        out_specs=q_spec,
        out_shape=jax.ShapeDtypeStruct((bsz, t, MIX_WIDTH), BF16),
        scratch_shapes=[pltpu.VMEM((HEADS_PER_TILE * tq, 2 * LANES), BF16),
                        pltpu.VMEM((HEADS_PER_TILE * tq, LANES), F32),
                        pltpu.VMEM((HEADS_PER_TILE * tq, 2 * LANES), F32),
                        pltpu.VMEM((HEADS_PER_TILE * tq, tq), F32),
                        pltpu.VMEM((HEADS_PER_TILE * tq, tq), F32)],
        compiler_params=_params("parallel", "parallel", "arbitrary"),
        name="mla_attn",
    )(qn, qr, kn, kr, v)


def _rwkv_proj_kernel(x_ref, w_in_ref, mu_ref, w0_ref, w2_ref, a0_ref, a2_ref, g2_ref, kk_ref, ka_ref,
                      rk_ref, seg_ref, r_o, lw_o, k_o, v_o, a_o, b_o, g_o, bonus_o, qm_o, carry_sc):
    n_groups = carry_sc.shape[0] // 8
    gm = x_ref.shape[1] // n_groups
    last_slot = slice(8 * (n_groups - 1), 8 * n_groups)

    @pl.when(pl.program_id(1) == 0)
    def _():
        carry_sc[last_slot, :] = jnp.zeros((8, carry_sc.shape[1]), F32)

    first_row = lax.broadcasted_iota(jnp.int32, (gm, 1), 0) == 0
    seg = seg_ref[...]
    projs = [_bdot(x_ref[0, gi * gm:(gi + 1) * gm, :], w_in_ref[...]) for gi in range(n_groups)]

    for gi, proj in enumerate(projs):
        rows = slice(gi * gm, (gi + 1) * gm)
        prev_slot = (gi - 1) % n_groups
        prev = carry_sc[8 * prev_slot + 7:8 * prev_slot + 8, :]
        carry_sc[8 * gi:8 * gi + 8, :] = proj[gm - 8:, :RW_TM_COLS]

        def mixed(lo, hi):
            cur = proj[:, lo:hi]
            shifted = jnp.where(first_row, prev[:, lo:hi], pltpu.roll(cur, 1, 0))
            return cur + (shifted - cur) * mu_ref[:, lo:hi]

        r = mixed(0, MIX_WIDTH)
        k = mixed(MIX_WIDTH, 2 * MIX_WIDTH)
        v = mixed(2 * MIX_WIDTH, RW_HW)
        hw = mixed(RW_HW, RW_HA)
        ha = mixed(RW_HA, RW_HG)
        hg = mixed(RW_HG, RW_TM_COLS)

        z = -(w0_ref[...] + _bdot(jnp.tanh(hw), w2_ref[...]))
        softplus = jnp.maximum(z, 0.0) + jnp.log(1.0 + jnp.exp(-jnp.abs(z)))
        log_decay = -jnp.exp(-softplus - 0.5)
        a = _sigmoid(a0_ref[...] + _bdot(ha, a2_ref[...]))
        gate = _bdot(_sigmoid(hg), g2_ref[...])

        kk = k * kk_ref[...]
        kk = kk / jnp.maximum(jnp.sqrt(_head_sums(kk * kk, seg)), 1e-12)
        k2 = k * (1.0 + (a - 1.0) * ka_ref[...])
        bonus = _head_sums(r * k2 * rk_ref[...], seg) * v

        r_o[0, rows, :] = r
        lw_o[0, rows, :] = log_decay
        k_o[0, rows, :] = k2
        v_o[0, rows, :] = v
        a_o[0, rows, :] = -kk
        b_o[0, rows, :] = kk * a
        g_o[0, rows, :] = gate
        bonus_o[0, rows, :] = bonus
        qm_o[0, rows, :] = (proj[:, RW_TM_COLS:] * MEM_SCALE).astype(BF16)


def _rwkv_proj(x, w_in, mu, w0, w2, a0, a2, g2, k_k, k_a, r_k, seg):
    bsz, t, d = x.shape
    tm = min(ROW_TILE, t)
    row = lambda width: pl.BlockSpec((1, tm, width), lambda b, i: (b, i, 0))
    wide = jax.ShapeDtypeStruct((bsz, t, MIX_WIDTH), F32)
    consts = (w_in, mu, w0, w2, a0, a2, g2, k_k, k_a, r_k, seg)
    return pl.pallas_call(
        _rwkv_proj_kernel,
        grid=(bsz, t // tm),
        in_specs=[row(d)] + [_const_spec(c.shape) for c in consts],
        out_specs=[row(MIX_WIDTH)] * 8 + [row(MEM_WIDTH)],
        out_shape=[wide] * 8 + [jax.ShapeDtypeStruct((bsz, t, MEM_WIDTH), BF16)],
        scratch_shapes=[pltpu.VMEM((8 * ROW_GROUPS, RW_TM_COLS), F32)],
        compiler_params=_params("parallel", "arbitrary"),
        name="rwkv_proj",
    )(x, *consts)


def _rwkv_scan_kernel(r_ref, lw_ref, k_ref, v_ref, a_ref, b_ref, y_ref, state_sc):
    c = SCAN_CHUNK
    n = HEADS_PER_TILE * c
    chunks = range(r_ref.shape[1] // c)

    @pl.when(pl.program_id(2) == 0)
    def _():
        state_sc[...] = jnp.zeros(state_sc.shape, F32)

    head0 = lax.broadcasted_iota(jnp.int32, (c, LANES), 1) < HEAD_DIM

    def stack(z):
        return jnp.concatenate([jnp.where(head0, z, 0.0), jnp.where(head0, 0.0, z)], axis=0)

    tril_ones = jnp.where(lax.broadcasted_iota(jnp.int32, (c, c), 0)
                          >= lax.broadcasted_iota(jnp.int32, (c, c), 1), 1.0, 0.0).astype(BF16)
    i2 = lax.broadcasted_iota(jnp.int32, (n, n), 0)
    j2 = lax.broadcasted_iota(jnp.int32, (n, n), 1)
    strict = i2 > j2
    incl = i2 >= j2

    rs, as_, bk, vs, w_end = [], [], [], [], []
    for ci in chunks:
        rows = slice(ci * c, (ci + 1) * c)
        lw = lw_ref[0, rows, :]
        lw_hi = lw.astype(BF16)
        lw_mid = (lw - lw_hi.astype(F32)).astype(BF16)
        lw_lo = (lw - lw_hi.astype(F32) - lw_mid.astype(F32)).astype(BF16)
        cum = (jnp.dot(tril_ones, lw_hi, preferred_element_type=F32)
               + jnp.dot(tril_ones, lw_mid, preferred_element_type=F32)
               + jnp.dot(tril_ones, lw_lo, preferred_element_type=F32))
        w_incl = jnp.exp(cum)
        w_inv = jnp.exp(-cum)
        rs.append(stack(r_ref[0, rows, :] * w_incl))
        as_.append(stack(a_ref[0, rows, :] * jnp.exp(cum - lw)))
        bk.append(jnp.concatenate([stack(b_ref[0, rows, :] * w_inv), stack(k_ref[0, rows, :] * w_inv)],
                                  axis=0).astype(BF16))
        vs.append(stack(v_ref[0, rows, :]))
        w_end.append(w_incl[c - 1:c, :])

    m = [_bdot_nt(jnp.concatenate([as_[ci], rs[ci]], axis=0), bk[ci]) for ci in chunks]
    a_pow = [jnp.where(strict, m[ci][:n, :n], 0.0) for ci in chunks]
    inv_m1 = a_pow
    reach = 2
    while reach < c:
        a_pow = [_bdot(a_pow[ci], a_pow[ci]) for ci in chunks]
        inv_m1 = [inv_m1[ci] + a_pow[ci] + _bdot(inv_m1[ci], a_pow[ci]) for ci in chunks]
        reach *= 2
    x = [jnp.concatenate([as_[ci], _bdot(jnp.where(strict, m[ci][:n, n:], 0.0), vs[ci])], axis=1)
         for ci in chunks]
    x = [x[ci] + _bdot(inv_m1[ci], x[ci]) for ci in chunks]
    z = [_bdot(jnp.where(incl, m[ci][n:, :n], 0.0), x[ci]) for ci in chunks]
    r_eff = [rs[ci] + z[ci][:, :LANES] for ci in chunks]
    y_loc = [z[ci][:, LANES:] + _bdot(jnp.where(incl, m[ci][n:, n:], 0.0), vs[ci]) for ci in chunks]
    g = [_bdot(x[ci][:, :LANES].T, bk[ci][:n]) for ci in chunks]
    h = [_bdot(jnp.concatenate([x[ci][:, LANES:], vs[ci]], axis=0).T, bk[ci]) for ci in chunks]

    state = state_sc[...]
    for ci in chunks:
        y = _bdot_nt(r_eff[ci], state) + y_loc[ci]
        y_ref[0, ci * c:(ci + 1) * c, :] = y[:c] + y[c:]
        state = (state + _bdot(state, g[ci]) + h[ci]) * w_end[ci]
    state_sc[...] = state


def _rwkv_scan(r, lw, k, v, a, b):
    bsz, t, _ = r.shape
    tc = min(SCAN_CHUNK * SCAN_CHUNKS_PER_STEP, t)
    spec = pl.BlockSpec((1, tc, LANES), lambda bi, p, c: (bi, c, p))
    return pl.pallas_call(
        _rwkv_scan_kernel,
        grid=(bsz, N_PAIRS, t // tc),
        in_specs=[spec] * 6,
        out_specs=spec,
        out_shape=jax.ShapeDtypeStruct((bsz, t, MIX_WIDTH), F32),
        scratch_shapes=[pltpu.VMEM((LANES, LANES), F32)],
        compiler_params=_params("parallel", "parallel", "arbitrary"),
        name="rwkv_scan",
    )(r, lw, k, v, a, b)


def _tail_kernel(*refs, rwkv, alpha):
    if rwkv:
        y_ref, bonus_ref, gate_ref, gng_ref, gnb_ref, seg_ref = refs[:6]
        refs = refs[6:]
    else:
        y_ref = refs[0]
        refs = refs[1:]
    (qm_ref, mk_ref, mv_ref, x_ref, wo_mix_ref, wo_mem_ref, ln1g_ref, ln1b_ref,
     w1_ref, w2_ref, ln2g_ref, ln2b_ref, o_ref) = refs

    m_tokens = mk_ref.shape[1] // MEM_HEADS
    tm = x_ref.shape[1]
    groups = [slice(i * (tm // ROW_GROUPS), (i + 1) * (tm // ROW_GROUPS)) for i in range(ROW_GROUPS)]

    def mixed_and_normed(rows):
        if rwkv:
            y = y_ref[0, rows, :]
            seg = seg_ref[...]
            mean = _head_sums(y, seg) * (1.0 / HEAD_DIM)
            d = y - mean
            var = _head_sums(d * d, seg) * (1.0 / HEAD_DIM)
            yn = d * lax.rsqrt(var + RWKV_GN_EPS) * gng_ref[...] + gnb_ref[...]
            o_mix = ((yn + bonus_ref[0, rows, :]) * gate_ref[0, rows, :]).astype(BF16)
        else:
            o_mix = y_ref[0, rows, :]
        s = lax.dot_general(qm_ref[0, rows, :], mk_ref[0], (((1,), (1,)), ((), ())),
                            preferred_element_type=F32)
        probs = []
        for h in range(MEM_HEADS):
            sh = s[:, h * m_tokens:(h + 1) * m_tokens]
            e = jnp.exp(sh - jnp.max(sh, axis=1, keepdims=True))
            probs.append((e / jnp.sum(e, axis=1, keepdims=True)).astype(BF16))
        o_mem = jnp.dot(jnp.concatenate(probs, axis=1), mv_ref[0], preferred_element_type=F32)
        mix = (jnp.dot(o_mix, wo_mix_ref[...], preferred_element_type=F32)
               + _bdot(o_mem, wo_mem_ref[...]))
        return _layer_norm(alpha * x_ref[0, rows, :] + mix, ln1g_ref[...], ln1b_ref[...])

    def mlp(x1):
        hidden = jnp.dot(x1.astype(BF16), w1_ref[...], preferred_element_type=F32)
        hidden = jnp.square(jnp.maximum(hidden, 0.0)).astype(BF16)
        ff = jnp.dot(hidden, w2_ref[...], preferred_element_type=F32)
        return _layer_norm(alpha * x1 + ff, ln2g_ref[...], ln2b_ref[...])

    x1 = [mixed_and_normed(rows) for rows in groups]
    for rows, x1_rows in zip(groups, x1):
        o_ref[0, rows, :] = mlp(x1_rows)


def _tail(mix_inputs, qm, mem_k, mem_v, x, wo_mix, wo_mem, ln1g, ln1b, w1, w2, ln2g, ln2b, *, rwkv, alpha):
    bsz, t, d = x.shape
    tm = min(ROW_TILE, t)
    row = lambda width: pl.BlockSpec((1, tm, width), lambda b, i: (b, i, 0))
    mem_spec = pl.BlockSpec((1,) + mem_k.shape[1:], lambda b, i: (b, 0, 0))
    single = lambda a: pl.BlockSpec(a.shape, lambda b, i: (0,) * a.ndim, pipeline_mode=pl.Buffered(1))
    if rwkv:
        y, bonus, gate, gng, gnb, seg = mix_inputs
        head_args = [y, bonus, gate, gng, gnb, seg]
        head_specs = [row(MIX_WIDTH)] * 3 + [single(gng), single(gnb), single(seg)]
    else:
        head_args = list(mix_inputs)
        head_specs = [row(MIX_WIDTH)]
    consts = (wo_mix, wo_mem, ln1g, ln1b, w1, w2, ln2g, ln2b)
    return pl.pallas_call(
        functools.partial(_tail_kernel, rwkv=rwkv, alpha=alpha),
        grid=(bsz, t // tm),
        in_specs=head_specs + [row(MEM_WIDTH), mem_spec, mem_spec, row(d)] + [single(c) for c in consts],
        out_specs=row(d),
        out_shape=jax.ShapeDtypeStruct((bsz, t, d), F32),
        compiler_params=_params("parallel", "parallel"),
        name="rwkv_tail" if rwkv else "mla_tail",
    )(*head_args, qm, mem_k, mem_v, x, *consts)


def _swap_halves(w):
    half = w.shape[-1] // 2
    return jnp.concatenate([w[..., half:], w[..., :half]], axis=-1)


def _pair_pack(w):
    kdim = w.shape[0]
    w = w.reshape(kdim, N_PAIRS, HEADS_PER_TILE * MLA_ROPE_DIM)
    pad = jnp.zeros((kdim, N_PAIRS, LANES - HEADS_PER_TILE * MLA_ROPE_DIM), w.dtype)
    return jnp.concatenate([w, pad], axis=-1).reshape(kdim, N_PAIRS * LANES)


def _mla_weights(w_in, w_q_up, w_kv_up):
    d = w_in.shape[0]
    o_kv, o_kr, o_qm = MLA_Q_RANK, MLA_Q_RANK + MLA_KV_RANK, MLA_Q_RANK + MLA_KV_RANK + MLA_ROPE_DIM
    w_kr = w_in[:, o_kr:o_qm]
    pad = jnp.zeros((d, LANES - 2 * MLA_ROPE_DIM), w_in.dtype)
    w_in_p = jnp.concatenate([w_in[:, :o_kr], w_in[:, o_qm:], w_kr, w_kr, pad,
                              _swap_halves(w_kr), _swap_halves(w_kr), pad], axis=1)
    wq = w_q_up.reshape(MLA_Q_RANK, MIX_HEADS, MLA_QK_DIM)
    rope = wq[:, :, MLA_NOPE_DIM:]
    w_q = jnp.concatenate([wq[:, :, :MLA_NOPE_DIM].reshape(MLA_Q_RANK, MIX_WIDTH),
                           _pair_pack(rope), _pair_pack(_swap_halves(rope))], axis=1)
    wkv = w_kv_up.reshape(MLA_KV_RANK, MIX_HEADS, 2 * HEAD_DIM)
    w_kv = jnp.concatenate([wkv[:, :, :HEAD_DIM].reshape(MLA_KV_RANK, MIX_WIDTH),
                            wkv[:, :, HEAD_DIM:].reshape(MLA_KV_RANK, MIX_WIDTH)], axis=1)
    return w_in_p.astype(BF16), w_q.astype(BF16), w_kv.astype(BF16)


def _rope_tables(positions):
    inv_freq = ROPE_BASE ** (-jnp.arange(ROPE_HALF, dtype=F32) * 2.0 / MLA_ROPE_DIM)
    ang = positions.astype(F32).reshape(-1, 1) * inv_freq
    cos, sin = jnp.cos(ang), jnp.sin(ang)
    pad = jnp.zeros((ang.shape[0], LANES - 2 * MLA_ROPE_DIM), F32)
    return (jnp.concatenate([cos, cos, cos, cos, pad], axis=1),
            jnp.concatenate([-sin, sin, -sin, sin, pad], axis=1))


def _pad_cols(w, width):
    return jnp.concatenate([w, jnp.zeros(w.shape[:-1] + (width - w.shape[-1],), w.dtype)], axis=-1)


def _pad_rows(w, height):
    return jnp.concatenate([w, jnp.zeros((height - w.shape[0],) + w.shape[1:], w.dtype)], axis=0)


def _rwkv_weights(w_in, mu, w2, a2, g2):
    o_hw, o_ha = 3 * MIX_WIDTH, 3 * MIX_WIDTH + RWKV_DECAY_RANK
    o_hg = o_ha + RWKV_A_RANK
    o_qm = o_hg + RWKV_GATE_RANK

    def layout(w):
        return jnp.concatenate([w[..., :o_hw], _pad_cols(w[..., o_hw:o_ha], LANES),
                                _pad_cols(w[..., o_ha:o_hg], LANES),
                                _pad_cols(w[..., o_hg:o_qm], 2 * LANES)], axis=-1)

    w_in_p = jnp.concatenate([layout(w_in[:, :o_qm]), w_in[:, o_qm:]], axis=1)
    return (w_in_p.astype(BF16), layout(mu).reshape(1, RW_TM_COLS),
            _pad_rows(w2, LANES).astype(BF16), _pad_rows(a2, LANES).astype(BF16),
            _pad_rows(g2, 2 * LANES).astype(BF16))


def kernel(x, mem, positions, mem_ln_g, mem_ln_b, w_mem_kv, mla_w_in, mla_q_norm, mla_w_q_up, mla_kv_norm, mla_w_kv_up, rwkv_w_in, rwkv_mu, rwkv_w0, rwkv_w2, rwkv_a0, rwkv_a2, rwkv_g2, rwkv_k_k, rwkv_k_a, rwkv_r_k, rwkv_gn_g, rwkv_gn_b, w_out, ln1_g, ln1_b, w_ff1, w_ff2, ln2_g, ln2_b):
    bsz, t, d = x.shape
    depth = w_out.shape[0]
    alpha = (2.0 * depth) ** 0.25
    row = lambda p: p.reshape(1, -1)

    mem_k, mem_v = _mem_kv(mem, mem_ln_g, mem_ln_b, w_mem_kv.astype(BF16))
    cos, sin = _rope_tables(positions)
    head_of = jnp.arange(MXU_DIM) // HEAD_DIM
    seg = (head_of[:, None] == head_of[None, :]).astype(BF16)

    for i in range(depth):
        j = i // 2
        tail_w = (w_out[i, :MIX_WIDTH].astype(BF16), w_out[i, MIX_WIDTH:].astype(BF16),
                  row(ln1_g[i]), row(ln1_b[i]), w_ff1[i].astype(BF16), w_ff2[i].astype(BF16),
                  row(ln2_g[i]), row(ln2_b[i]))
        if i % 2 == 0:
            w_in, w_q, w_kv = _mla_weights(mla_w_in[j], mla_w_q_up[j], mla_w_kv_up[j])
            qn, qr, kn, kr, v, qm = _mla_proj(x.reshape(bsz * t, d), cos, sin, w_in, row(mla_q_norm[j]), w_q,
                                              row(mla_kv_norm[j]), w_kv)
            seq = lambda a: a.reshape(bsz, t, a.shape[-1])
            o_mix = _mla_attn(seq(qn), seq(qr), seq(kn), seq(kr), seq(v))
            x = _tail((o_mix,), seq(qm), mem_k, mem_v, x, *tail_w, rwkv=False, alpha=alpha)
        else:
            w_in, mu, w2, a2, g2 = _rwkv_weights(rwkv_w_in[j], rwkv_mu[j], rwkv_w2[j], rwkv_a2[j], rwkv_g2[j])
            r, lw, k, v, a, b, gate, bonus, qm = _rwkv_proj(
                x, w_in, mu, row(rwkv_w0[j]), w2, row(rwkv_a0[j]), a2, g2, row(rwkv_k_k[j]), row(rwkv_k_a[j]),
                row(rwkv_r_k[j]), seg)
            y = _rwkv_scan(r, lw, k, v, a, b)
            x = _tail((y, bonus, gate, row(rwkv_gn_g[j]), row(rwkv_gn_b[j]), seg), qm, mem_k, mem_v, x,
                      *tail_w, rwkv=True, alpha=alpha)
    return x
```

```python
import functools

import jax
import jax.numpy as jnp
from jax import lax
from jax.experimental import pallas as pl
from jax.experimental.pallas import tpu as pltpu

F32 = jnp.float32
BF16 = jnp.bfloat16

HEAD_DIM = 64
MIX_HEADS = 12
MEM_HEADS = 4
MIX_WIDTH = MIX_HEADS * HEAD_DIM
MEM_WIDTH = MEM_HEADS * HEAD_DIM
MLA_Q_RANK = 384
MLA_KV_RANK = 256
MLA_NOPE_DIM = 64
MLA_ROPE_DIM = 32
MLA_QK_DIM = MLA_NOPE_DIM + MLA_ROPE_DIM
ROPE_HALF = MLA_ROPE_DIM // 2
ROPE_BASE = 10000.0
RWKV_DECAY_RANK = 64
RWKV_A_RANK = 64
RWKV_GATE_RANK = 160
RWKV_GN_EPS = 64e-5
LN_EPS = 1e-5
RMS_EPS = 1e-6
QK_SCALE = MLA_QK_DIM ** -0.5 * 1.4426950408889634
MEM_SCALE = HEAD_DIM ** -0.5
MASK_VALUE = -1e30

LANES = 128
MXU_DIM = 256
HEADS_PER_TILE = LANES // HEAD_DIM
N_PAIRS = MIX_HEADS // HEADS_PER_TILE
VMEM_LIMIT_BYTES = 56 * 1024 * 1024

ROW_TILE = 512
ATTN_TILE = 512
ATTN_ROW_BLOCKS = 4
SCAN_CHUNK = 64
SCAN_CHUNKS_PER_STEP = 4
SCAN_PAIRS_PER_STEP = 6
ROW_GROUPS = 2

RW_HW = 3 * MIX_WIDTH
RW_HA = RW_HW + LANES
RW_HG = RW_HA + LANES
RW_TM_COLS = RW_HG + 2 * LANES
RW_COLS = RW_TM_COLS + MEM_WIDTH

ML_CKV = MLA_Q_RANK
ML_QMEM = ML_CKV + MLA_KV_RANK
ML_KR = ML_QMEM + MEM_WIDTH
ML_KR_SWAP = ML_KR + LANES
ML_COLS = ML_KR_SWAP + LANES


def _bdot(a, b):
    return jnp.dot(a.astype(BF16), b.astype(BF16), preferred_element_type=F32)


def _bdot_nt(a, b):
    return lax.dot_general(a.astype(BF16), b.astype(BF16), (((1,), (1,)), ((), ())),
                           preferred_element_type=F32)


def _head_sums(x, seg):
    hi = x.astype(BF16)
    lo = (x - hi.astype(F32)).astype(BF16)
    parts = []
    for g in range(x.shape[1] // MXU_DIM):
        cols = slice(g * MXU_DIM, (g + 1) * MXU_DIM)
        parts.append(jnp.dot(hi[:, cols], seg, preferred_element_type=F32)
                     + jnp.dot(lo[:, cols], seg, preferred_element_type=F32))
    return jnp.concatenate(parts, axis=1)


def _layer_norm(t, g, b):
    mu = jnp.mean(t, axis=-1, keepdims=True)
    d = t - mu
    var = jnp.mean(d * d, axis=-1, keepdims=True)
    return d * lax.rsqrt(var + LN_EPS) * g + b


def _rms_norm(t, g):
    return t * lax.rsqrt(jnp.mean(t * t, axis=-1, keepdims=True) + RMS_EPS) * g


def _sigmoid(z):
    return 1.0 / (1.0 + jnp.exp(-z))


def _params(*semantics):
    return pltpu.CompilerParams(dimension_semantics=semantics, vmem_limit_bytes=VMEM_LIMIT_BYTES)


def _const_spec(shape):
    nd = len(shape)
    return pl.BlockSpec(shape, lambda *_: (0,) * nd)


def _mem_kv_kernel(mem_ref, g_ref, b_ref, w_ref, k_ref, v_ref):
    m_tokens = mem_ref.shape[1]
    kv = _bdot(_layer_norm(mem_ref[0], g_ref[...], b_ref[...]), w_ref[...])
    k = kv[:, :MEM_WIDTH]
    v = kv[:, MEM_WIDTH:]
    head_of_lane = jnp.right_shift(lax.broadcasted_iota(jnp.int32, k.shape, 1), HEAD_DIM.bit_length() - 1)
    for h in range(MEM_HEADS):
        rows = slice(h * m_tokens, (h + 1) * m_tokens)
        k_ref[0, rows, :] = jnp.where(head_of_lane == h, k, 0.0).astype(BF16)
        v_ref[0, rows, :] = jnp.where(head_of_lane == h, v, 0.0).astype(BF16)


def _mem_kv(mem, g, b, w):
    bsz, m_tokens, d = mem.shape
    out = jax.ShapeDtypeStruct((bsz, MEM_HEADS * m_tokens, MEM_WIDTH), BF16)
    blk = pl.BlockSpec((1, MEM_HEADS * m_tokens, MEM_WIDTH), lambda i: (i, 0, 0))
    return pl.pallas_call(
        _mem_kv_kernel,
        grid=(bsz,),
        in_specs=[pl.BlockSpec((1, m_tokens, d), lambda i: (i, 0, 0)),
                  _const_spec((1, d)), _const_spec((1, d)), _const_spec(w.shape)],
        out_specs=[blk, blk],
        out_shape=[out, out],
        compiler_params=_params("parallel"),
        name="mem_kv",
    )(mem, g.reshape(1, d), b.reshape(1, d), w)


def _mla_proj_kernel(x_ref, cos_ref, sin_ref, w_in_ref, qg_ref, wq_ref, kvg_ref, wkv_ref,
                     qn_ref, qr_ref, kn_ref, kr_ref, v_ref, qm_ref):
    c = _bdot(x_ref[...], w_in_ref[...])
    q = _bdot(_rms_norm(c[:, :ML_CKV], qg_ref[...]), wq_ref[...])
    kv = _bdot(_rms_norm(c[:, ML_CKV:ML_QMEM], kvg_ref[...]), wkv_ref[...])
    cos = cos_ref[...]
    sin = sin_ref[...]
    cos_all = jnp.concatenate([cos] * N_PAIRS, axis=1)
    sin_all = jnp.concatenate([sin] * N_PAIRS, axis=1)
    qn_ref[...] = (q[:, :MIX_WIDTH] * QK_SCALE).astype(BF16)
    q_rot = q[:, MIX_WIDTH:2 * MIX_WIDTH] * cos_all + q[:, 2 * MIX_WIDTH:] * sin_all
    qr_ref[...] = (q_rot * QK_SCALE).astype(BF16)
    kn_ref[...] = kv[:, :MIX_WIDTH].astype(BF16)
    v_ref[...] = kv[:, MIX_WIDTH:].astype(BF16)
    kr_ref[...] = (c[:, ML_KR:ML_KR_SWAP] * cos + c[:, ML_KR_SWAP:] * sin).astype(BF16)
    qm_ref[...] = (c[:, ML_QMEM:ML_KR] * MEM_SCALE).astype(BF16)


def _mla_proj(x2d, cos, sin, w_in, q_norm, w_q, kv_norm, w_kv):
    n, d = x2d.shape
    tm = min(ROW_TILE, n)
    row = lambda width: pl.BlockSpec((tm, width), lambda i: (i, 0))
    shape = lambda width: jax.ShapeDtypeStruct((n, width), BF16)
    return pl.pallas_call(
        _mla_proj_kernel,
        grid=(n // tm,),
        in_specs=[row(d), row(LANES), row(LANES), _const_spec(w_in.shape), _const_spec(q_norm.shape),
                  _const_spec(w_q.shape), _const_spec(kv_norm.shape), _const_spec(w_kv.shape)],
        out_specs=[row(MIX_WIDTH), row(MIX_WIDTH), row(MIX_WIDTH), row(LANES), row(MIX_WIDTH), row(MEM_WIDTH)],
        out_shape=[shape(MIX_WIDTH), shape(MIX_WIDTH), shape(MIX_WIDTH), shape(LANES), shape(MIX_WIDTH),
                   shape(MEM_WIDTH)],
        compiler_params=_params("parallel"),
        name="mla_proj",
    )(x2d, cos, sin, w_in, q_norm, w_q, kv_norm, w_kv)


def _mla_attn_kernel(qn_ref, qr_ref, kn_ref, kr_ref, v_ref, o_ref, q_sc, m_sc, acc_sc, sa_sc, sb_sc):
    tq = qn_ref.shape[1]
    qi = pl.program_id(2)
    lane = lax.broadcasted_iota(jnp.int32, (tq, LANES), 1)
    qn = qn_ref[0]
    qr = qr_ref[0]
    zero = jnp.zeros_like(qn)
    q_sc[:tq, :LANES] = jnp.where(lane < HEAD_DIM, qn, zero)
    q_sc[:tq, LANES:] = jnp.where(lane < MLA_ROPE_DIM, qr, zero)
    q_sc[tq:, :LANES] = jnp.where(lane >= HEAD_DIM, qn, zero)
    q_sc[tq:, LANES:] = jnp.where(lane >= MLA_ROPE_DIM, qr, zero)
    m_sc[...] = jnp.full(m_sc.shape, -jnp.inf, F32)
    acc_sc[...] = jnp.zeros(acc_sc.shape, F32)

    ones = jnp.ones((tq, LANES), BF16)
    rb = HEADS_PER_TILE * tq // ATTN_ROW_BLOCKS

    def scores(j, s_ref):
        rows = pl.ds(pl.multiple_of(j * tq, tq), tq)
        k = jnp.concatenate([kn_ref[0, rows, :], kr_ref[0, rows, :]], axis=1)
        s_ref[...] = lax.dot_general(q_sc[...], k, (((1,), (1,)), ((), ())), preferred_element_type=F32)

    def consume(j, s_ref, masked):
        rows = pl.ds(pl.multiple_of(j * tq, tq), tq)
        v_ones = jnp.concatenate([v_ref[0, rows, :], ones], axis=1)
        for blk in range(ATTN_ROW_BLOCKS):
            brows = slice(blk * rb, (blk + 1) * rb)
            s = s_ref[brows, :]
            if masked:
                q_pos = lax.broadcasted_iota(jnp.int32, s.shape, 0) + (blk * rb) % tq
                k_pos = lax.broadcasted_iota(jnp.int32, s.shape, 1)
                s = jnp.where(k_pos <= q_pos, s, MASK_VALUE)
            m_prev = m_sc[brows, :]
            m_new = jnp.maximum(m_prev, jnp.max(s, axis=1, keepdims=True))
            alpha = jnp.exp2(m_prev - m_new)
            p = jnp.exp2(s - jnp.concatenate([m_new] * (tq // LANES), axis=1))
            acc_sc[brows, :] = (jnp.concatenate([alpha, alpha], axis=1) * acc_sc[brows, :]
                                + jnp.dot(p.astype(BF16), v_ones, preferred_element_type=F32))
            m_sc[brows, :] = m_new

    scores(0, sa_sc)

    def body(i, carry):
        scores(2 * i + 1, sb_sc)
        consume(2 * i, sa_sc, False)
        scores(2 * i + 2, sa_sc)
        consume(2 * i + 1, sb_sc, False)
        return carry

    lax.fori_loop(0, qi // 2, body, 0)

    @pl.when(qi % 2 == 0)
    def _():
        consume(qi, sa_sc, True)

    @pl.when(qi % 2 == 1)
    def _():
        scores(qi, sb_sc)
        consume(qi - 1, sa_sc, False)
        consume(qi, sb_sc, True)

    out = acc_sc[:, :LANES] / acc_sc[:, LANES:]
    o_ref[0] = jnp.where(lane < HEAD_DIM, out[:tq], out[tq:]).astype(BF16)


def _mla_attn(qn, qr, kn, kr, v):
    bsz, t, _ = qn.shape
    tq = min(ATTN_TILE, t)
    assert tq & (tq - 1) == 0 and t % tq == 0
    q_spec = pl.BlockSpec((1, tq, LANES), lambda b, p, i: (b, i, p))
    seq_spec = pl.BlockSpec((1, t, LANES), lambda b, p, i: (b, 0, p))
    return pl.pallas_call(
        _mla_attn_kernel,
        grid=(bsz, N_PAIRS, t // tq),
        in_specs=[q_spec, q_spec, seq_spec, pl.BlockSpec((1, t, LANES), lambda b, p, i: (b, 0, 0)), seq_spec],
        out_specs=q_spec,
        out_shape=jax.ShapeDtypeStruct((bsz, t, MIX_WIDTH), BF16),
        scratch_shapes=[pltpu.VMEM((HEADS_PER_TILE * tq, 2 * LANES), BF16),
                        pltpu.VMEM((HEADS_PER_TILE * tq, LANES), F32),
                        pltpu.VMEM((HEADS_PER_TILE * tq, 2 * LANES), F32),
                        pltpu.VMEM((HEADS_PER_TILE * tq, tq), F32),
                        pltpu.VMEM((HEADS_PER_TILE * tq, tq), F32)],
        compiler_params=_params("parallel", "parallel", "arbitrary"),
        name="mla_attn",
    )(qn, qr, kn, kr, v)


def _rwkv_proj_kernel(x_ref, w_in_ref, mu_ref, w0_ref, w2_ref, a0_ref, a2_ref, g2_ref, kk_ref, ka_ref,
                      rk_ref, seg_ref, r_o, lw_o, k_o, v_o, a_o, b_o, g_o, bonus_o, qm_o, carry_sc):
    n_groups = carry_sc.shape[0] // 8
    gm = x_ref.shape[1] // n_groups
    last_slot = slice(8 * (n_groups - 1), 8 * n_groups)

    @pl.when(pl.program_id(1) == 0)
    def _():
        carry_sc[last_slot, :] = jnp.zeros((8, carry_sc.shape[1]), F32)

    first_row = lax.broadcasted_iota(jnp.int32, (gm, 1), 0) == 0
    seg = seg_ref[...]
    projs = [_bdot(x_ref[0, gi * gm:(gi + 1) * gm, :], w_in_ref[...]) for gi in range(n_groups)]

    for gi, proj in enumerate(projs):
        rows = slice(gi * gm, (gi + 1) * gm)
        prev_slot = (gi - 1) % n_groups
        prev = carry_sc[8 * prev_slot + 7:8 * prev_slot + 8, :]
        carry_sc[8 * gi:8 * gi + 8, :] = proj[gm - 8:, :RW_TM_COLS]

        def mixed(lo, hi):
            cur = proj[:, lo:hi]
            shifted = jnp.where(first_row, prev[:, lo:hi], pltpu.roll(cur, 1, 0))
            return cur + (shifted - cur) * mu_ref[:, lo:hi]

        r = mixed(0, MIX_WIDTH)
        k = mixed(MIX_WIDTH, 2 * MIX_WIDTH)
        v = mixed(2 * MIX_WIDTH, RW_HW)
        hw = mixed(RW_HW, RW_HA)
        ha = mixed(RW_HA, RW_HG)
        hg = mixed(RW_HG, RW_TM_COLS)

        z = -(w0_ref[...] + _bdot(jnp.tanh(hw), w2_ref[...]))
        softplus = jnp.maximum(z, 0.0) + jnp.log(1.0 + jnp.exp(-jnp.abs(z)))
        log_decay = -jnp.exp(-softplus - 0.5)
        a = _sigmoid(a0_ref[...] + _bdot(ha, a2_ref[...]))
        gate = _bdot(_sigmoid(hg), g2_ref[...])

        kk = k * kk_ref[...]
        kk = kk / jnp.maximum(jnp.sqrt(_head_sums(kk * kk, seg)), 1e-12)
        k2 = k * (1.0 + (a - 1.0) * ka_ref[...])
        bonus = _head_sums(r * k2 * rk_ref[...], seg) * v

        r_o[0, rows, :] = r
        lw_o[0, rows, :] = log_decay
        k_o[0, rows, :] = k2
        v_o[0, rows, :] = v
        a_o[0, rows, :] = -kk
        b_o[0, rows, :] = kk * a
        g_o[0, rows, :] = gate
        bonus_o[0, rows, :] = bonus
        qm_o[0, rows, :] = (proj[:, RW_TM_COLS:] * MEM_SCALE).astype(BF16)


def _rwkv_proj(x, w_in, mu, w0, w2, a0, a2, g2, k_k, k_a, r_k, seg):
    bsz, t, d = x.shape
    tm = min(ROW_TILE, t)
    row = lambda width: pl.BlockSpec((1, tm, width), lambda b, i: (b, i, 0))
    wide = jax.ShapeDtypeStruct((bsz, t, MIX_WIDTH), F32)
    consts = (w_in, mu, w0, w2, a0, a2, g2, k_k, k_a, r_k, seg)
    return pl.pallas_call(
        _rwkv_proj_kernel,
        grid=(bsz, t // tm),
        in_specs=[row(d)] + [_const_spec(c.shape) for c in consts],
        out_specs=[row(MIX_WIDTH)] * 8 + [row(MEM_WIDTH)],
        out_shape=[wide] * 8 + [jax.ShapeDtypeStruct((bsz, t, MEM_WIDTH), BF16)],
        scratch_shapes=[pltpu.VMEM((8 * ROW_GROUPS, RW_TM_COLS), F32)],
        compiler_params=_params("parallel", "arbitrary"),
        name="rwkv_proj",
    )(x, *consts)


def _rwkv_scan_kernel(r_ref, lw_ref, k_ref, v_ref, a_ref, b_ref, y_ref, state_sc):
    c = SCAN_CHUNK
    n = HEADS_PER_TILE * c
    n_pairs = r_ref.shape[2] // LANES
    items = [(p, cj) for cj in range(r_ref.shape[1] // c) for p in range(n_pairs)]
    chunks = range(len(items))

    @pl.when(pl.program_id(2) == 0)
    def _():
        state_sc[...] = jnp.zeros(state_sc.shape, F32)

    head0 = lax.broadcasted_iota(jnp.int32, (c, LANES), 1) < HEAD_DIM

    def stack(z):
        return jnp.concatenate([jnp.where(head0, z, 0.0), jnp.where(head0, 0.0, z)], axis=0)

    def stack_bf16(z):
        return stack(z).astype(BF16)

    tril_ones = jnp.where(lax.broadcasted_iota(jnp.int32, (c, c), 0)
                          >= lax.broadcasted_iota(jnp.int32, (c, c), 1), 1.0, 0.0).astype(BF16)
    t_row = lax.broadcasted_iota(jnp.int32, (c, LANES), 0)
    s_col = jnp.bitwise_and(lax.broadcasted_iota(jnp.int32, (c, LANES), 1), HEAD_DIM - 1)
    strict = t_row > s_col
    incl = t_row >= s_col

    rt, at, bk, vs, w_end = [], [], [], [], []
    for p, cj in items:
        rows = slice(cj * c, (cj + 1) * c)
        lanes = slice(p * LANES, (p + 1) * LANES)
        lw = lw_ref[0, rows, lanes]
        lw_hi = lw.astype(BF16)
        lw_mid = (lw - lw_hi.astype(F32)).astype(BF16)
        lw_lo = (lw - lw_hi.astype(F32) - lw_mid.astype(F32)).astype(BF16)
        cum = (jnp.dot(tril_ones, lw_hi, preferred_element_type=F32)
               + jnp.dot(tril_ones, lw_mid, preferred_element_type=F32)
               + jnp.dot(tril_ones, lw_lo, preferred_element_type=F32))
        w_incl = jnp.exp(cum)
        w_inv = jnp.exp(-cum)
        rt.append(r_ref[0, rows, lanes] * w_incl)
        at.append(a_ref[0, rows, lanes] * jnp.exp(cum - lw))
        bk.append(jnp.concatenate([stack(b_ref[0, rows, lanes] * w_inv), stack(k_ref[0, rows, lanes] * w_inv)],
                                  axis=0).astype(BF16))
        vs.append(stack(v_ref[0, rows, lanes]))
        w_end.append(w_incl[c - 1:c, :])

    m = [_bdot_nt(jnp.concatenate([at[ci], rt[ci]], axis=0), bk[ci]) for ci in chunks]
    a_pow = [jnp.where(strict, m[ci][:c, :n], 0.0) for ci in chunks]
    inv_m1 = a_pow
    reach = 2
    while reach < c:
        pow_s = [stack_bf16(a_pow[ci]) for ci in chunks]
        a_pow = [_bdot(a_pow[ci], pow_s[ci]) for ci in chunks]
        pow_s = [stack_bf16(a_pow[ci]) for ci in chunks]
        inv_m1 = [inv_m1[ci] + a_pow[ci] + _bdot(inv_m1[ci], pow_s[ci]) for ci in chunks]
        reach *= 2
    akv = [_bdot(jnp.where(strict, m[ci][:c, n:], 0.0), vs[ci]) for ci in chunks]
    x = [jnp.concatenate([at[ci], akv[ci]], axis=1)
         + _bdot(inv_m1[ci], jnp.concatenate([stack(at[ci]), stack(akv[ci])], axis=1))
         for ci in chunks]
    xs = [jnp.concatenate([stack(x[ci][:, :LANES]), stack(x[ci][:, LANES:])], axis=1)
          for ci in chunks]
    z = [_bdot(jnp.where(incl, m[ci][c:, :n], 0.0), xs[ci]) for ci in chunks]
    r_eff = [rt[ci] + z[ci][:, :LANES] for ci in chunks]
    y_loc = [z[ci][:, LANES:] + _bdot(jnp.where(incl, m[ci][c:, n:], 0.0), vs[ci]) for ci in chunks]
    g = [_bdot(xs[ci][:, :LANES].T, bk[ci][:n]) for ci in chunks]
    h = [_bdot(jnp.concatenate([xs[ci][:, LANES:], vs[ci]], axis=0).T, bk[ci]) for ci in chunks]

    states = [state_sc[p] for p in range(n_pairs)]
    for ci, (p, cj) in enumerate(items):
        y_ref[0, cj * c:(cj + 1) * c, p * LANES:(p + 1) * LANES] = _bdot_nt(r_eff[ci], states[p]) + y_loc[ci]
        states[p] = (states[p] + _bdot(states[p], g[ci]) + h[ci]) * w_end[ci]
    for p in range(n_pairs):
        state_sc[p] = states[p]


def _rwkv_scan(r, lw, k, v, a, b):
    bsz, t, _ = r.shape
    tc = min(SCAN_CHUNK * SCAN_CHUNKS_PER_STEP, t)
    spec = pl.BlockSpec((1, tc, SCAN_PAIRS_PER_STEP * LANES), lambda bi, p, c: (bi, c, p))
    return pl.pallas_call(
        _rwkv_scan_kernel,
        grid=(bsz, N_PAIRS // SCAN_PAIRS_PER_STEP, t // tc),
        in_specs=[spec] * 6,
        out_specs=spec,
        out_shape=jax.ShapeDtypeStruct((bsz, t, MIX_WIDTH), F32),
        scratch_shapes=[pltpu.VMEM((SCAN_PAIRS_PER_STEP, LANES, LANES), F32)],
        compiler_params=_params("parallel", "parallel", "arbitrary"),
        name="rwkv_scan",
    )(r, lw, k, v, a, b)


def _tail_kernel(*refs, rwkv, alpha):
    if rwkv:
        y_ref, bonus_ref, gate_ref, gng_ref, gnb_ref, seg_ref = refs[:6]
        refs = refs[6:]
    else:
        y_ref = refs[0]
        refs = refs[1:]
    (qm_ref, mk_ref, mv_ref, x_ref, wo_mix_ref, wo_mem_ref, ln1g_ref, ln1b_ref,
     w1_ref, w2_ref, ln2g_ref, ln2b_ref, o_ref) = refs

    m_tokens = mk_ref.shape[1] // MEM_HEADS
    tm = x_ref.shape[1]
    groups = [slice(i * (tm // ROW_GROUPS), (i + 1) * (tm // ROW_GROUPS)) for i in range(ROW_GROUPS)]

    def mixed_and_normed(rows):
        if rwkv:
            y = y_ref[0, rows, :]
            seg = seg_ref[...]
            mean = _head_sums(y, seg) * (1.0 / HEAD_DIM)
            d = y - mean
            var = _head_sums(d * d, seg) * (1.0 / HEAD_DIM)
            yn = d * lax.rsqrt(var + RWKV_GN_EPS) * gng_ref[...] + gnb_ref[...]
            o_mix = ((yn + bonus_ref[0, rows, :]) * gate_ref[0, rows, :]).astype(BF16)
        else:
            o_mix = y_ref[0, rows, :]
        s = lax.dot_general(qm_ref[0, rows, :], mk_ref[0], (((1,), (1,)), ((), ())),
                            preferred_element_type=F32)
        probs = []
        for h in range(MEM_HEADS):
            sh = s[:, h * m_tokens:(h + 1) * m_tokens]
            e = jnp.exp(sh - jnp.max(sh, axis=1, keepdims=True))
            probs.append((e / jnp.sum(e, axis=1, keepdims=True)).astype(BF16))
        o_mem = jnp.dot(jnp.concatenate(probs, axis=1), mv_ref[0], preferred_element_type=F32)
        mix = (jnp.dot(o_mix, wo_mix_ref[...], preferred_element_type=F32)
               + _bdot(o_mem, wo_mem_ref[...]))
        return _layer_norm(alpha * x_ref[0, rows, :] + mix, ln1g_ref[...], ln1b_ref[...])

    def mlp(x1):
        hidden = jnp.dot(x1.astype(BF16), w1_ref[...], preferred_element_type=F32)
        hidden = jnp.square(jnp.maximum(hidden, 0.0)).astype(BF16)
        ff = jnp.dot(hidden, w2_ref[...], preferred_element_type=F32)
        return _layer_norm(alpha * x1 + ff, ln2g_ref[...], ln2b_ref[...])

    x1 = [mixed_and_normed(rows) for rows in groups]
    for rows, x1_rows in zip(groups, x1):
        o_ref[0, rows, :] = mlp(x1_rows)


def _tail(mix_inputs, qm, mem_k, mem_v, x, wo_mix, wo_mem, ln1g, ln1b, w1, w2, ln2g, ln2b, *, rwkv, alpha):
    bsz, t, d = x.shape
    tm = min(ROW_TILE, t)
    row = lambda width: pl.BlockSpec((1, tm, width), lambda b, i: (b, i, 0))
    mem_spec = pl.BlockSpec((1,) + mem_k.shape[1:], lambda b, i: (b, 0, 0))
    single = lambda a: pl.BlockSpec(a.shape, lambda b, i: (0,) * a.ndim, pipeline_mode=pl.Buffered(1))
    if rwkv:
        y, bonus, gate, gng, gnb, seg = mix_inputs
        head_args = [y, bonus, gate, gng, gnb, seg]
        head_specs = [row(MIX_WIDTH)] * 3 + [single(gng), single(gnb), single(seg)]
    else:
        head_args = list(mix_inputs)
        head_specs = [row(MIX_WIDTH)]
    consts = (wo_mix, wo_mem, ln1g, ln1b, w1, w2, ln2g, ln2b)
    return pl.pallas_call(
        functools.partial(_tail_kernel, rwkv=rwkv, alpha=alpha),
        grid=(bsz, t // tm),
        in_specs=head_specs + [row(MEM_WIDTH), mem_spec, mem_spec, row(d)] + [single(c) for c in consts],
        out_specs=row(d),
        out_shape=jax.ShapeDtypeStruct((bsz, t, d), F32),
        compiler_params=_params("parallel", "parallel"),
        name="rwkv_tail" if rwkv else "mla_tail",
    )(*head_args, qm, mem_k, mem_v, x, *consts)


def _swap_halves(w):
    half = w.shape[-1] // 2
    return jnp.concatenate([w[..., half:], w[..., :half]], axis=-1)


def _pair_pack(w):
    kdim = w.shape[0]
    w = w.reshape(kdim, N_PAIRS, HEADS_PER_TILE * MLA_ROPE_DIM)
    pad = jnp.zeros((kdim, N_PAIRS, LANES - HEADS_PER_TILE * MLA_ROPE_DIM), w.dtype)
    return jnp.concatenate([w, pad], axis=-1).reshape(kdim, N_PAIRS * LANES)


def _mla_weights(w_in, w_q_up, w_kv_up):
    d = w_in.shape[0]
    o_kv, o_kr, o_qm = MLA_Q_RANK, MLA_Q_RANK + MLA_KV_RANK, MLA_Q_RANK + MLA_KV_RANK + MLA_ROPE_DIM
    w_kr = w_in[:, o_kr:o_qm]
    pad = jnp.zeros((d, LANES - 2 * MLA_ROPE_DIM), w_in.dtype)
    w_in_p = jnp.concatenate([w_in[:, :o_kr], w_in[:, o_qm:], w_kr, w_kr, pad,
                              _swap_halves(w_kr), _swap_halves(w_kr), pad], axis=1)
    wq = w_q_up.reshape(MLA_Q_RANK, MIX_HEADS, MLA_QK_DIM)
    rope = wq[:, :, MLA_NOPE_DIM:]
    w_q = jnp.concatenate([wq[:, :, :MLA_NOPE_DIM].reshape(MLA_Q_RANK, MIX_WIDTH),
                           _pair_pack(rope), _pair_pack(_swap_halves(rope))], axis=1)
    wkv = w_kv_up.reshape(MLA_KV_RANK, MIX_HEADS, 2 * HEAD_DIM)
    w_kv = jnp.concatenate([wkv[:, :, :HEAD_DIM].reshape(MLA_KV_RANK, MIX_WIDTH),
                            wkv[:, :, HEAD_DIM:].reshape(MLA_KV_RANK, MIX_WIDTH)], axis=1)
    return w_in_p.astype(BF16), w_q.astype(BF16), w_kv.astype(BF16)


def _rope_tables(positions):
    lane = jnp.arange(LANES)
    inv_freq = ROPE_BASE ** (-(lane % ROPE_HALF).astype(F32) * 2.0 / MLA_ROPE_DIM)
    ang = positions.astype(F32).reshape(-1, 1) * inv_freq
    used = lane < HEADS_PER_TILE * MLA_ROPE_DIM
    sign = jnp.where((lane // ROPE_HALF) % 2 == 0, -1.0, 1.0).astype(F32)
    return jnp.where(used, jnp.cos(ang), 0.0), jnp.where(used, sign * jnp.sin(ang), 0.0)


def _pad_cols(w, width):
    return jnp.concatenate([w, jnp.zeros(w.shape[:-1] + (width - w.shape[-1],), w.dtype)], axis=-1)


def _pad_rows(w, height):
    return jnp.concatenate([w, jnp.zeros((height - w.shape[0],) + w.shape[1:], w.dtype)], axis=0)


def _rwkv_weights(w_in, mu, w2, a2, g2):
    o_hw, o_ha = 3 * MIX_WIDTH, 3 * MIX_WIDTH + RWKV_DECAY_RANK
    o_hg = o_ha + RWKV_A_RANK
    o_qm = o_hg + RWKV_GATE_RANK

    def layout(w):
        return jnp.concatenate([w[..., :o_hw], _pad_cols(w[..., o_hw:o_ha], LANES),
                                _pad_cols(w[..., o_ha:o_hg], LANES),
                                _pad_cols(w[..., o_hg:o_qm], 2 * LANES)], axis=-1)

    w_in_p = jnp.concatenate([layout(w_in[:, :o_qm]), w_in[:, o_qm:]], axis=1)
    return (w_in_p.astype(BF16), layout(mu).reshape(1, RW_TM_COLS),
            _pad_rows(w2, LANES).astype(BF16), _pad_rows(a2, LANES).astype(BF16),
            _pad_rows(g2, 2 * LANES).astype(BF16))


def kernel(x, mem, positions, mem_ln_g, mem_ln_b, w_mem_kv, mla_w_in, mla_q_norm, mla_w_q_up, mla_kv_norm, mla_w_kv_up, rwkv_w_in, rwkv_mu, rwkv_w0, rwkv_w2, rwkv_a0, rwkv_a2, rwkv_g2, rwkv_k_k, rwkv_k_a, rwkv_r_k, rwkv_gn_g, rwkv_gn_b, w_out, ln1_g, ln1_b, w_ff1, w_ff2, ln2_g, ln2_b):
    bsz, t, d = x.shape
    depth = w_out.shape[0]
    alpha = (2.0 * depth) ** 0.25
    row = lambda p: p.reshape(1, -1)

    mem_k, mem_v = _mem_kv(mem, mem_ln_g, mem_ln_b, w_mem_kv.astype(BF16))
    cos, sin = _rope_tables(positions)
    head_of = jnp.arange(MXU_DIM) // HEAD_DIM
    seg = (head_of[:, None] == head_of[None, :]).astype(BF16)

    for i in range(depth):
        j = i // 2
        tail_w = (w_out[i, :MIX_WIDTH].astype(BF16), w_out[i, MIX_WIDTH:].astype(BF16),
                  row(ln1_g[i]), row(ln1_b[i]), w_ff1[i].astype(BF16), w_ff2[i].astype(BF16),
                  row(ln2_g[i]), row(ln2_b[i]))
        if i % 2 == 0:
            w_in, w_q, w_kv = _mla_weights(mla_w_in[j], mla_w_q_up[j], mla_w_kv_up[j])
            qn, qr, kn, kr, v, qm = _mla_proj(x.reshape(bsz * t, d), cos, sin, w_in, row(mla_q_norm[j]), w_q,
                                              row(mla_kv_norm[j]), w_kv)
            seq = lambda a: a.reshape(bsz, t, a.shape[-1])
            o_mix = _mla_attn(seq(qn), seq(qr), seq(kn), seq(kr), seq(v))
            x = _tail((o_mix,), seq(qm), mem_k, mem_v, x, *tail_w, rwkv=False, alpha=alpha)
        else:
            w_in, mu, w2, a2, g2 = _rwkv_weights(rwkv_w_in[j], rwkv_mu[j], rwkv_w2[j], rwkv_a2[j], rwkv_g2[j])
            r, lw, k, v, a, b, gate, bonus, qm = _rwkv_proj(
                x, w_in, mu, row(rwkv_w0[j]), w2, row(rwkv_a0[j]), a2, g2, row(rwkv_k_k[j]), row(rwkv_k_a[j]),
                row(rwkv_r_k[j]), seg)
            y = _rwkv_scan(r, lw, k, v, a, b)
            x = _tail((y, bonus, gate, row(rwkv_gn_g[j]), row(rwkv_gn_b[j]), seg), qm, mem_k, mem_v, x,
                      *tail_w, rwkv=True, alpha=alpha)
    return x
```

```python
import functools

import jax
import jax.numpy as jnp
from jax import lax
from jax.experimental import pallas as pl
from jax.experimental.pallas import tpu as pltpu

F32 = jnp.float32
BF16 = jnp.bfloat16

HEAD_DIM = 64
MIX_HEADS = 12
MEM_HEADS = 4
MIX_WIDTH = MIX_HEADS * HEAD_DIM
MEM_WIDTH = MEM_HEADS * HEAD_DIM
MLA_Q_RANK = 384
MLA_KV_RANK = 256
MLA_NOPE_DIM = 64
MLA_ROPE_DIM = 32
MLA_QK_DIM = MLA_NOPE_DIM + MLA_ROPE_DIM
ROPE_HALF = MLA_ROPE_DIM // 2
ROPE_BASE = 10000.0
RWKV_DECAY_RANK = 64
RWKV_A_RANK = 64
RWKV_GATE_RANK = 160
RWKV_GN_EPS = 64e-5
LN_EPS = 1e-5
RMS_EPS = 1e-6
QK_SCALE = MLA_QK_DIM ** -0.5 * 1.4426950408889634
MEM_SCALE = HEAD_DIM ** -0.5
MASK_VALUE = -1e30

LANES = 128
MXU_DIM = 256
HEADS_PER_TILE = LANES // HEAD_DIM
N_PAIRS = MIX_HEADS // HEADS_PER_TILE
VMEM_LIMIT_BYTES = 56 * 1024 * 1024

ROW_TILE = 512
ATTN_TILE = 512
ATTN_ROW_BLOCKS = 4
SCAN_CHUNK = 64
SCAN_CHUNKS_PER_STEP = 4
SCAN_PAIRS_PER_STEP = 6
ROW_GROUPS = 2

RW_HW = 3 * MIX_WIDTH
RW_HA = RW_HW + LANES
RW_HG = RW_HA + LANES
RW_TM_COLS = RW_HG + 2 * LANES
RW_COLS = RW_TM_COLS + MEM_WIDTH

ML_CKV = MLA_Q_RANK
ML_QMEM = ML_CKV + MLA_KV_RANK
ML_KR = ML_QMEM + MEM_WIDTH
ML_KR_SWAP = ML_KR + LANES
ML_COLS = ML_KR_SWAP + LANES


def _bdot(a, b):
    return jnp.dot(a.astype(BF16), b.astype(BF16), preferred_element_type=F32)


def _bdot_nt(a, b):
    return lax.dot_general(a.astype(BF16), b.astype(BF16), (((1,), (1,)), ((), ())),
                           preferred_element_type=F32)


def _head_sums(x, seg):
    hi = x.astype(BF16)
    lo = (x - hi.astype(F32)).astype(BF16)
    parts = []
    for g in range(x.shape[1] // MXU_DIM):
        cols = slice(g * MXU_DIM, (g + 1) * MXU_DIM)
        parts.append(jnp.dot(hi[:, cols], seg, preferred_element_type=F32)
                     + jnp.dot(lo[:, cols], seg, preferred_element_type=F32))
    return jnp.concatenate(parts, axis=1)


def _layer_norm(t, g, b):
    mu = jnp.mean(t, axis=-1, keepdims=True)
    d = t - mu
    var = jnp.mean(d * d, axis=-1, keepdims=True)
    return d * lax.rsqrt(var + LN_EPS) * g + b


def _rms_norm(t, g):
    return t * lax.rsqrt(jnp.mean(t * t, axis=-1, keepdims=True) + RMS_EPS) * g


def _sigmoid(z):
    return 1.0 / (1.0 + jnp.exp(-z))


def _params(*semantics):
    return pltpu.CompilerParams(dimension_semantics=semantics, vmem_limit_bytes=VMEM_LIMIT_BYTES)


def _const_spec(shape):
    nd = len(shape)
    return pl.BlockSpec(shape, lambda *_: (0,) * nd)


def _mem_kv_kernel(mem_ref, g_ref, b_ref, w_ref, k_ref, v_ref):
    m_tokens = mem_ref.shape[1]
    kv = _bdot(_layer_norm(mem_ref[0], g_ref[...], b_ref[...]), w_ref[...])
    k = kv[:, :MEM_WIDTH]
    v = kv[:, MEM_WIDTH:]
    head_of_lane = jnp.right_shift(lax.broadcasted_iota(jnp.int32, k.shape, 1), HEAD_DIM.bit_length() - 1)
    for h in range(MEM_HEADS):
        rows = slice(h * m_tokens, (h + 1) * m_tokens)
        k_ref[0, rows, :] = jnp.where(head_of_lane == h, k, 0.0).astype(BF16)
        v_ref[0, rows, :] = jnp.where(head_of_lane == h, v, 0.0).astype(BF16)


def _mem_kv(mem, g, b, w):
    bsz, m_tokens, d = mem.shape
    out = jax.ShapeDtypeStruct((bsz, MEM_HEADS * m_tokens, MEM_WIDTH), BF16)
    blk = pl.BlockSpec((1, MEM_HEADS * m_tokens, MEM_WIDTH), lambda i: (i, 0, 0))
    return pl.pallas_call(
        _mem_kv_kernel,
        grid=(bsz,),
        in_specs=[pl.BlockSpec((1, m_tokens, d), lambda i: (i, 0, 0)),
                  _const_spec((1, d)), _const_spec((1, d)), _const_spec(w.shape)],
        out_specs=[blk, blk],
        out_shape=[out, out],
        compiler_params=_params("parallel"),
        name="mem_kv",
    )(mem, g.reshape(1, d), b.reshape(1, d), w)


def _mla_proj_kernel(x_ref, cos_ref, sin_ref, w_in_ref, qg_ref, wq_ref, kvg_ref, wkv_ref,
                     qn_ref, qr_ref, kn_ref, kr_ref, v_ref, qm_ref):
    c = _bdot(x_ref[...], w_in_ref[...])
    q = _bdot(_rms_norm(c[:, :ML_CKV], qg_ref[...]), wq_ref[...])
    kv = _bdot(_rms_norm(c[:, ML_CKV:ML_QMEM], kvg_ref[...]), wkv_ref[...])
    cos = cos_ref[...]
    sin = sin_ref[...]
    cos_all = jnp.concatenate([cos] * N_PAIRS, axis=1)
    sin_all = jnp.concatenate([sin] * N_PAIRS, axis=1)
    qn_ref[...] = (q[:, :MIX_WIDTH] * QK_SCALE).astype(BF16)
    q_rot = q[:, MIX_WIDTH:2 * MIX_WIDTH] * cos_all + q[:, 2 * MIX_WIDTH:] * sin_all
    qr_ref[...] = (q_rot * QK_SCALE).astype(BF16)
    kn_ref[...] = kv[:, :MIX_WIDTH].astype(BF16)
    v_ref[...] = kv[:, MIX_WIDTH:].astype(BF16)
    kr_ref[...] = (c[:, ML_KR:ML_KR_SWAP] * cos + c[:, ML_KR_SWAP:] * sin).astype(BF16)
    qm_ref[...] = (c[:, ML_QMEM:ML_KR] * MEM_SCALE).astype(BF16)


def _mla_proj(x2d, cos, sin, w_in, q_norm, w_q, kv_norm, w_kv):
    n, d = x2d.shape
    tm = min(ROW_TILE, n)
    row = lambda width: pl.BlockSpec((tm, width), lambda i: (i, 0))
    shape = lambda width: jax.ShapeDtypeStruct((n, width), BF16)
    return pl.pallas_call(
        _mla_proj_kernel,
        grid=(n // tm,),
        in_specs=[row(d), row(LANES), row(LANES), _const_spec(w_in.shape), _const_spec(q_norm.shape),
                  _const_spec(w_q.shape), _const_spec(kv_norm.shape), _const_spec(w_kv.shape)],
        out_specs=[row(MIX_WIDTH), row(MIX_WIDTH), row(MIX_WIDTH), row(LANES), row(MIX_WIDTH), row(MEM_WIDTH)],
        out_shape=[shape(MIX_WIDTH), shape(MIX_WIDTH), shape(MIX_WIDTH), shape(LANES), shape(MIX_WIDTH),
                   shape(MEM_WIDTH)],
        compiler_params=_params("parallel"),
        name="mla_proj",
    )(x2d, cos, sin, w_in, q_norm, w_q, kv_norm, w_kv)


def _mla_attn_kernel(qn_ref, qr_ref, kn_ref, kr_ref, v_ref, o_ref, q_sc, m_sc, acc_sc, sa_sc, sb_sc):
    tq = qn_ref.shape[1] // 2
    mi = pl.program_id(2)
    lane = lax.broadcasted_iota(jnp.int32, (tq, LANES), 1)
    for slot in range(2):
        qn = qn_ref[0, slot * tq:(slot + 1) * tq, :]
        qr = qr_ref[0, slot * tq:(slot + 1) * tq, :]
        zero = jnp.zeros_like(qn)
        q_sc[slot, :tq, :LANES] = jnp.where(lane < HEAD_DIM, qn, zero)
        q_sc[slot, :tq, LANES:] = jnp.where(lane < MLA_ROPE_DIM, qr, zero)
        q_sc[slot, tq:, :LANES] = jnp.where(lane >= HEAD_DIM, qn, zero)
        q_sc[slot, tq:, LANES:] = jnp.where(lane >= MLA_ROPE_DIM, qr, zero)
    m_sc[...] = jnp.full(m_sc.shape, -jnp.inf, F32)
    acc_sc[...] = jnp.zeros(acc_sc.shape, F32)

    ones = jnp.ones((tq, LANES), BF16)
    rb = HEADS_PER_TILE * tq // ATTN_ROW_BLOCKS

    def scores(slot, j, s_ref):
        rows = pl.ds(pl.multiple_of(j * tq, tq), tq)
        k = jnp.concatenate([kn_ref[0, rows, :], kr_ref[0, rows, :]], axis=1)
        s_ref[...] = lax.dot_general(q_sc[slot], k, (((1,), (1,)), ((), ())), preferred_element_type=F32)

    def consume(slot, j, s_ref, masked):
        rows = pl.ds(pl.multiple_of(j * tq, tq), tq)
        v_ones = jnp.concatenate([v_ref[0, rows, :], ones], axis=1)
        for blk in range(ATTN_ROW_BLOCKS):
            brows = slice(blk * rb, (blk + 1) * rb)
            s = s_ref[brows, :]
            if masked:
                q_pos = lax.broadcasted_iota(jnp.int32, s.shape, 0) + (blk * rb) % tq
                k_pos = lax.broadcasted_iota(jnp.int32, s.shape, 1)
                s = jnp.where(k_pos <= q_pos, s, MASK_VALUE)
            m_prev = m_sc[slot, brows, :]
            m_new = jnp.maximum(m_prev, jnp.max(s, axis=1, keepdims=True))
            alpha = jnp.exp2(m_prev - m_new)
            p = jnp.exp2(s - jnp.concatenate([m_new] * (tq // LANES), axis=1))
            acc_sc[slot, brows, :] = (jnp.concatenate([alpha, alpha], axis=1) * acc_sc[slot, brows, :]
                                      + jnp.dot(p.astype(BF16), v_ones, preferred_element_type=F32))
            m_sc[slot, brows, :] = m_new

    def finish(slot):
        out = acc_sc[slot, :, :LANES] / acc_sc[slot, :, LANES:]
        o_ref[0, slot * tq:(slot + 1) * tq, :] = jnp.where(lane < HEAD_DIM, out[:tq], out[tq:]).astype(BF16)

    def two_tiles(slot, first, second):
        def body(i, carry):
            scores(slot, 2 * i + 1, second)
            consume(slot, 2 * i, first, False)
            scores(slot, 2 * i + 2, first)
            consume(slot, 2 * i + 1, second, False)
            return carry
        return body

    scores(0, 0, sa_sc)
    lax.fori_loop(0, mi, two_tiles(0, sa_sc, sb_sc), 0)
    scores(1, 0, sb_sc)
    consume(0, 2 * mi, sa_sc, True)
    finish(0)
    lax.fori_loop(0, mi, two_tiles(1, sb_sc, sa_sc), 0)
    scores(1, 2 * mi + 1, sa_sc)
    consume(1, 2 * mi, sb_sc, False)
    consume(1, 2 * mi + 1, sa_sc, True)
    finish(1)


def _mla_attn(qn, qr, kn, kr, v):
    bsz, t, _ = qn.shape
    tq = min(ATTN_TILE, t)
    assert tq & (tq - 1) == 0 and t % (2 * tq) == 0
    q_spec = pl.BlockSpec((1, 2 * tq, LANES), lambda b, p, i: (b, i, p))
    seq_spec = pl.BlockSpec((1, t, LANES), lambda b, p, i: (b, 0, p))
    return pl.pallas_call(
        _mla_attn_kernel,
        grid=(bsz, N_PAIRS, t // (2 * tq)),
        in_specs=[q_spec, q_spec, seq_spec, pl.BlockSpec((1, t, LANES), lambda b, p, i: (b, 0, 0)), seq_spec],
        out_specs=q_spec,
        out_shape=jax.ShapeDtypeStruct((bsz, t, MIX_WIDTH), BF16),
        scratch_shapes=[pltpu.VMEM((2, HEADS_PER_TILE * tq, 2 * LANES), BF16),
                        pltpu.VMEM((2, HEADS_PER_TILE * tq, LANES), F32),
                        pltpu.VMEM((2, HEADS_PER_TILE * tq, 2 * LANES), F32),
                        pltpu.VMEM((HEADS_PER_TILE * tq, tq), F32),
                        pltpu.VMEM((HEADS_PER_TILE * tq, tq), F32)],
        compiler_params=_params("parallel", "parallel", "arbitrary"),
        name="mla_attn",
    )(qn, qr, kn, kr, v)


def _rwkv_proj_kernel(x_ref, w_in_ref, mu_ref, w0_ref, w2_ref, a0_ref, a2_ref, g2_ref, kk_ref, ka_ref,
                      rk_ref, seg_ref, r_o, lw_o, k_o, v_o, a_o, b_o, g_o, bonus_o, qm_o, carry_sc):
    n_groups = carry_sc.shape[0] // 8
    gm = x_ref.shape[1] // n_groups
    last_slot = slice(8 * (n_groups - 1), 8 * n_groups)

    @pl.when(pl.program_id(1) == 0)
    def _():
        carry_sc[last_slot, :] = jnp.zeros((8, carry_sc.shape[1]), F32)

    first_row = lax.broadcasted_iota(jnp.int32, (gm, 1), 0) == 0
    seg = seg_ref[...]
    projs = [_bdot(x_ref[0, gi * gm:(gi + 1) * gm, :], w_in_ref[...]) for gi in range(n_groups)]

    for gi, proj in enumerate(projs):
        rows = slice(gi * gm, (gi + 1) * gm)
        prev_slot = (gi - 1) % n_groups
        prev = carry_sc[8 * prev_slot + 7:8 * prev_slot + 8, :]
        carry_sc[8 * gi:8 * gi + 8, :] = proj[gm - 8:, :RW_TM_COLS]

        def mixed(lo, hi):
            cur = proj[:, lo:hi]
            shifted = jnp.where(first_row, prev[:, lo:hi], pltpu.roll(cur, 1, 0))
            return cur + (shifted - cur) * mu_ref[:, lo:hi]

        r = mixed(0, MIX_WIDTH)
        k = mixed(MIX_WIDTH, 2 * MIX_WIDTH)
        v = mixed(2 * MIX_WIDTH, RW_HW)
        hw = mixed(RW_HW, RW_HA)
        ha = mixed(RW_HA, RW_HG)
        hg = mixed(RW_HG, RW_TM_COLS)

        z = -(w0_ref[...] + _bdot(jnp.tanh(hw), w2_ref[...]))
        softplus = jnp.maximum(z, 0.0) + jnp.log(1.0 + jnp.exp(-jnp.abs(z)))
        log_decay = -jnp.exp(-softplus - 0.5)
        a = _sigmoid(a0_ref[...] + _bdot(ha, a2_ref[...]))
        gate = _bdot(_sigmoid(hg), g2_ref[...])

        kk = k * kk_ref[...]
        kk = kk / jnp.maximum(jnp.sqrt(_head_sums(kk * kk, seg)), 1e-12)
        k2 = k * (1.0 + (a - 1.0) * ka_ref[...])
        bonus = _head_sums(r * k2 * rk_ref[...], seg) * v

        r_o[0, rows, :] = r
        lw_o[0, rows, :] = log_decay
        k_o[0, rows, :] = k2
        v_o[0, rows, :] = v.astype(BF16)
        a_o[0, rows, :] = -kk
        b_o[0, rows, :] = kk * a
        g_o[0, rows, :] = gate
        bonus_o[0, rows, :] = bonus
        qm_o[0, rows, :] = (proj[:, RW_TM_COLS:] * MEM_SCALE).astype(BF16)


def _rwkv_proj(x, w_in, mu, w0, w2, a0, a2, g2, k_k, k_a, r_k, seg):
    bsz, t, d = x.shape
    tm = min(ROW_TILE, t)
    row = lambda width: pl.BlockSpec((1, tm, width), lambda b, i: (b, i, 0))
    wide = jax.ShapeDtypeStruct((bsz, t, MIX_WIDTH), F32)
    consts = (w_in, mu, w0, w2, a0, a2, g2, k_k, k_a, r_k, seg)
    return pl.pallas_call(
        _rwkv_proj_kernel,
        grid=(bsz, t // tm),
        in_specs=[row(d)] + [_const_spec(c.shape) for c in consts],
        out_specs=[row(MIX_WIDTH)] * 8 + [row(MEM_WIDTH)],
        out_shape=[wide] * 3 + [jax.ShapeDtypeStruct((bsz, t, MIX_WIDTH), BF16)] + [wide] * 4
        + [jax.ShapeDtypeStruct((bsz, t, MEM_WIDTH), BF16)],
        scratch_shapes=[pltpu.VMEM((8 * ROW_GROUPS, RW_TM_COLS), F32)],
        compiler_params=_params("parallel", "arbitrary"),
        name="rwkv_proj",
    )(x, *consts)


def _rwkv_scan_kernel(r_ref, lw_ref, k_ref, v_ref, a_ref, b_ref, y_ref, state_sc):
    c = SCAN_CHUNK
    n = HEADS_PER_TILE * c
    n_pairs = r_ref.shape[2] // LANES
    items = [(p, cj) for cj in range(r_ref.shape[1] // c) for p in range(n_pairs)]
    chunks = range(len(items))

    @pl.when(pl.program_id(2) == 0)
    def _():
        state_sc[...] = jnp.zeros(state_sc.shape, F32)

    head0 = lax.broadcasted_iota(jnp.int32, (c, LANES), 1) < HEAD_DIM

    def stack(z):
        return jnp.concatenate([jnp.where(head0, z, 0.0), jnp.where(head0, 0.0, z)], axis=0)

    def stack_bf16(z):
        return stack(z).astype(BF16)

    tril_ones = jnp.where(lax.broadcasted_iota(jnp.int32, (c, c), 0)
                          >= lax.broadcasted_iota(jnp.int32, (c, c), 1), 1.0, 0.0).astype(BF16)
    t_row = lax.broadcasted_iota(jnp.int32, (c, LANES), 0)
    s_col = jnp.bitwise_and(lax.broadcasted_iota(jnp.int32, (c, LANES), 1), HEAD_DIM - 1)
    strict = t_row > s_col
    incl = t_row >= s_col

    rt, at, bk, vs, w_end = [], [], [], [], []
    for p, cj in items:
        rows = slice(cj * c, (cj + 1) * c)
        lanes = slice(p * LANES, (p + 1) * LANES)
        lw = lw_ref[0, rows, lanes]
        lw_hi = lw.astype(BF16)
        lw_mid = (lw - lw_hi.astype(F32)).astype(BF16)
        lw_lo = (lw - lw_hi.astype(F32) - lw_mid.astype(F32)).astype(BF16)
        cum = (jnp.dot(tril_ones, lw_hi, preferred_element_type=F32)
               + jnp.dot(tril_ones, lw_mid, preferred_element_type=F32)
               + jnp.dot(tril_ones, lw_lo, preferred_element_type=F32))
        w_incl = jnp.exp(cum)
        w_inv = jnp.exp(-cum)
        rt.append(r_ref[0, rows, lanes] * w_incl)
        at.append(a_ref[0, rows, lanes] * jnp.exp(cum - lw))
        bk.append(jnp.concatenate([stack(b_ref[0, rows, lanes] * w_inv), stack(k_ref[0, rows, lanes] * w_inv)],
                                  axis=0).astype(BF16))
        vs.append(stack(v_ref[0, rows, lanes]))
        w_end.append(w_incl[c - 1:c, :])

    m = [_bdot_nt(jnp.concatenate([at[ci], rt[ci]], axis=0), bk[ci]) for ci in chunks]
    a_pow = [jnp.where(strict, m[ci][:c, :n], 0.0) for ci in chunks]
    inv_m1 = a_pow
    reach = 2
    while reach < c:
        pow_s = [stack_bf16(a_pow[ci]) for ci in chunks]
        a_pow = [_bdot(a_pow[ci], pow_s[ci]) for ci in chunks]
        pow_s = [stack_bf16(a_pow[ci]) for ci in chunks]
        inv_m1 = [inv_m1[ci] + a_pow[ci] + _bdot(inv_m1[ci], pow_s[ci]) for ci in chunks]
        reach *= 2
    akv = [_bdot(jnp.where(strict, m[ci][:c, n:], 0.0), vs[ci]) for ci in chunks]
    x = [jnp.concatenate([at[ci], akv[ci]], axis=1)
         + _bdot(inv_m1[ci], jnp.concatenate([stack(at[ci]), stack(akv[ci])], axis=1))
         for ci in chunks]
    xs = [jnp.concatenate([stack(x[ci][:, :LANES]), stack(x[ci][:, LANES:])], axis=1)
          for ci in chunks]
    z = [_bdot(jnp.where(incl, m[ci][c:, :n], 0.0), xs[ci]) for ci in chunks]
    r_eff = [rt[ci] + z[ci][:, :LANES] for ci in chunks]
    y_loc = [z[ci][:, LANES:] + _bdot(jnp.where(incl, m[ci][c:, n:], 0.0), vs[ci]) for ci in chunks]
    g = [_bdot(xs[ci][:, :LANES].T, bk[ci][:n]) for ci in chunks]
    h = [_bdot(jnp.concatenate([xs[ci][:, LANES:], vs[ci]], axis=0).T, bk[ci]) for ci in chunks]

    states = [state_sc[p] for p in range(n_pairs)]
    for ci, (p, cj) in enumerate(items):
        y_ref[0, cj * c:(cj + 1) * c, p * LANES:(p + 1) * LANES] = _bdot_nt(r_eff[ci], states[p]) + y_loc[ci]
        states[p] = (states[p] + _bdot(states[p], g[ci]) + h[ci]) * w_end[ci]
    for p in range(n_pairs):
        state_sc[p] = states[p]


def _rwkv_scan(r, lw, k, v, a, b):
    bsz, t, _ = r.shape
    tc = min(SCAN_CHUNK * SCAN_CHUNKS_PER_STEP, t)
    spec = pl.BlockSpec((1, tc, SCAN_PAIRS_PER_STEP * LANES), lambda bi, p, c: (bi, c, p))
    return pl.pallas_call(
        _rwkv_scan_kernel,
        grid=(bsz, N_PAIRS // SCAN_PAIRS_PER_STEP, t // tc),
        in_specs=[spec] * 6,
        out_specs=spec,
        out_shape=jax.ShapeDtypeStruct((bsz, t, MIX_WIDTH), F32),
        scratch_shapes=[pltpu.VMEM((SCAN_PAIRS_PER_STEP, LANES, LANES), F32)],
        compiler_params=_params("parallel", "parallel", "arbitrary"),
        name="rwkv_scan",
    )(r, lw, k, v, a, b)


def _tail_kernel(*refs, rwkv, alpha):
    if rwkv:
        y_ref, bonus_ref, gate_ref, gng_ref, gnb_ref, seg_ref = refs[:6]
        refs = refs[6:]
    else:
        y_ref = refs[0]
        refs = refs[1:]
    (qm_ref, mk_ref, mv_ref, x_ref, wo_mix_ref, wo_mem_ref, ln1g_ref, ln1b_ref,
     w1_ref, w2_ref, ln2g_ref, ln2b_ref, o_ref) = refs

    m_tokens = mk_ref.shape[1] // MEM_HEADS
    tm = x_ref.shape[1]
    groups = [slice(i * (tm // ROW_GROUPS), (i + 1) * (tm // ROW_GROUPS)) for i in range(ROW_GROUPS)]

    def mixed_and_normed(rows):
        if rwkv:
            y = y_ref[0, rows, :]
            seg = seg_ref[...]
            mean = _head_sums(y, seg) * (1.0 / HEAD_DIM)
            d = y - mean
            var = _head_sums(d * d, seg) * (1.0 / HEAD_DIM)
            yn = d * lax.rsqrt(var + RWKV_GN_EPS) * gng_ref[...] + gnb_ref[...]
            o_mix = ((yn + bonus_ref[0, rows, :]) * gate_ref[0, rows, :]).astype(BF16)
        else:
            o_mix = y_ref[0, rows, :]
        s = lax.dot_general(qm_ref[0, rows, :], mk_ref[0], (((1,), (1,)), ((), ())),
                            preferred_element_type=F32)
        probs = []
        for h in range(MEM_HEADS):
            sh = s[:, h * m_tokens:(h + 1) * m_tokens]
            e = jnp.exp(sh - jnp.max(sh, axis=1, keepdims=True))
            probs.append((e / jnp.sum(e, axis=1, keepdims=True)).astype(BF16))
        o_mem = jnp.dot(jnp.concatenate(probs, axis=1), mv_ref[0], preferred_element_type=F32)
        mix = (jnp.dot(o_mix, wo_mix_ref[...], preferred_element_type=F32)
               + _bdot(o_mem, wo_mem_ref[...]))
        return _layer_norm(alpha * x_ref[0, rows, :] + mix, ln1g_ref[...], ln1b_ref[...])

    def mlp(x1):
        hidden = jnp.dot(x1.astype(BF16), w1_ref[...], preferred_element_type=F32)
        hidden = jnp.square(jnp.maximum(hidden, 0.0)).astype(BF16)
        ff = jnp.dot(hidden, w2_ref[...], preferred_element_type=F32)
        return _layer_norm(alpha * x1 + ff, ln2g_ref[...], ln2b_ref[...])

    x1 = [mixed_and_normed(rows) for rows in groups]
    for rows, x1_rows in zip(groups, x1):
        o_ref[0, rows, :] = mlp(x1_rows)


def _tail(mix_inputs, qm, mem_k, mem_v, x, wo_mix, wo_mem, ln1g, ln1b, w1, w2, ln2g, ln2b, *, rwkv, alpha):
    bsz, t, d = x.shape
    tm = min(ROW_TILE, t)
    row = lambda width: pl.BlockSpec((1, tm, width), lambda b, i: (b, i, 0))
    mem_spec = pl.BlockSpec((1,) + mem_k.shape[1:], lambda b, i: (b, 0, 0))
    single = lambda a: pl.BlockSpec(a.shape, lambda b, i: (0,) * a.ndim, pipeline_mode=pl.Buffered(1))
    if rwkv:
        y, bonus, gate, gng, gnb, seg = mix_inputs
        head_args = [y, bonus, gate, gng, gnb, seg]
        head_specs = [row(MIX_WIDTH)] * 3 + [single(gng), single(gnb), single(seg)]
    else:
        head_args = list(mix_inputs)
        head_specs = [row(MIX_WIDTH)]
    consts = (wo_mix, wo_mem, ln1g, ln1b, w1, w2, ln2g, ln2b)
    return pl.pallas_call(
        functools.partial(_tail_kernel, rwkv=rwkv, alpha=alpha),
        grid=(bsz, t // tm),
        in_specs=head_specs + [row(MEM_WIDTH), mem_spec, mem_spec, row(d)] + [single(c) for c in consts],
        out_specs=row(d),
        out_shape=jax.ShapeDtypeStruct((bsz, t, d), F32),
        compiler_params=_params("parallel", "parallel"),
        name="rwkv_tail" if rwkv else "mla_tail",
    )(*head_args, qm, mem_k, mem_v, x, *consts)


def _swap_halves(w):
    half = w.shape[-1] // 2
    return jnp.concatenate([w[..., half:], w[..., :half]], axis=-1)


def _pair_pack(w):
    kdim = w.shape[0]
    w = w.reshape(kdim, N_PAIRS, HEADS_PER_TILE * MLA_ROPE_DIM)
    pad = jnp.zeros((kdim, N_PAIRS, LANES - HEADS_PER_TILE * MLA_ROPE_DIM), w.dtype)
    return jnp.concatenate([w, pad], axis=-1).reshape(kdim, N_PAIRS * LANES)


def _mla_weights(w_in, w_q_up, w_kv_up):
    d = w_in.shape[0]
    o_kv, o_kr, o_qm = MLA_Q_RANK, MLA_Q_RANK + MLA_KV_RANK, MLA_Q_RANK + MLA_KV_RANK + MLA_ROPE_DIM
    w_kr = w_in[:, o_kr:o_qm]
    pad = jnp.zeros((d, LANES - 2 * MLA_ROPE_DIM), w_in.dtype)
    w_in_p = jnp.concatenate([w_in[:, :o_kr], w_in[:, o_qm:], w_kr, w_kr, pad,
                              _swap_halves(w_kr), _swap_halves(w_kr), pad], axis=1)
    wq = w_q_up.reshape(MLA_Q_RANK, MIX_HEADS, MLA_QK_DIM)
    rope = wq[:, :, MLA_NOPE_DIM:]
    w_q = jnp.concatenate([wq[:, :, :MLA_NOPE_DIM].reshape(MLA_Q_RANK, MIX_WIDTH),
                           _pair_pack(rope), _pair_pack(_swap_halves(rope))], axis=1)
    wkv = w_kv_up.reshape(MLA_KV_RANK, MIX_HEADS, 2 * HEAD_DIM)
    w_kv = jnp.concatenate([wkv[:, :, :HEAD_DIM].reshape(MLA_KV_RANK, MIX_WIDTH),
                            wkv[:, :, HEAD_DIM:].reshape(MLA_KV_RANK, MIX_WIDTH)], axis=1)
    return w_in_p.astype(BF16), w_q.astype(BF16), w_kv.astype(BF16)


def _rope_tables(positions):
    lane = jnp.arange(LANES)
    inv_freq = ROPE_BASE ** (-(lane % ROPE_HALF).astype(F32) * 2.0 / MLA_ROPE_DIM)
    ang = positions.astype(F32).reshape(-1, 1) * inv_freq
    used = lane < HEADS_PER_TILE * MLA_ROPE_DIM
    sign = jnp.where((lane // ROPE_HALF) % 2 == 0, -1.0, 1.0).astype(F32)
    return jnp.where(used, jnp.cos(ang), 0.0), jnp.where(used, sign * jnp.sin(ang), 0.0)


def _pad_cols(w, width):
    return jnp.concatenate([w, jnp.zeros(w.shape[:-1] + (width - w.shape[-1],), w.dtype)], axis=-1)


def _pad_rows(w, height):
    return jnp.concatenate([w, jnp.zeros((height - w.shape[0],) + w.shape[1:], w.dtype)], axis=0)


def _rwkv_weights(w_in, mu, w2, a2, g2):
    o_hw, o_ha = 3 * MIX_WIDTH, 3 * MIX_WIDTH + RWKV_DECAY_RANK
    o_hg = o_ha + RWKV_A_RANK
    o_qm = o_hg + RWKV_GATE_RANK

    def layout(w):
        return jnp.concatenate([w[..., :o_hw], _pad_cols(w[..., o_hw:o_ha], LANES),
                                _pad_cols(w[..., o_ha:o_hg], LANES),
                                _pad_cols(w[..., o_hg:o_qm], 2 * LANES)], axis=-1)

    w_in_p = jnp.concatenate([layout(w_in[:, :o_qm]), w_in[:, o_qm:]], axis=1)
    return (w_in_p.astype(BF16), layout(mu).reshape(1, RW_TM_COLS),
            _pad_rows(w2, LANES).astype(BF16), _pad_rows(a2, LANES).astype(BF16),
            _pad_rows(g2, 2 * LANES).astype(BF16))


def kernel(x, mem, positions, mem_ln_g, mem_ln_b, w_mem_kv, mla_w_in, mla_q_norm, mla_w_q_up, mla_kv_norm, mla_w_kv_up, rwkv_w_in, rwkv_mu, rwkv_w0, rwkv_w2, rwkv_a0, rwkv_a2, rwkv_g2, rwkv_k_k, rwkv_k_a, rwkv_r_k, rwkv_gn_g, rwkv_gn_b, w_out, ln1_g, ln1_b, w_ff1, w_ff2, ln2_g, ln2_b):
    bsz, t, d = x.shape
    depth = w_out.shape[0]
    alpha = (2.0 * depth) ** 0.25
    row = lambda p: p.reshape(1, -1)

    mem_k, mem_v = _mem_kv(mem, mem_ln_g, mem_ln_b, w_mem_kv.astype(BF16))
    cos, sin = _rope_tables(positions)
    head_of = jnp.arange(MXU_DIM) // HEAD_DIM
    seg = (head_of[:, None] == head_of[None, :]).astype(BF16)

    for i in range(depth):
        j = i // 2
        tail_w = (w_out[i, :MIX_WIDTH].astype(BF16), w_out[i, MIX_WIDTH:].astype(BF16),
                  row(ln1_g[i]), row(ln1_b[i]), w_ff1[i].astype(BF16), w_ff2[i].astype(BF16),
                  row(ln2_g[i]), row(ln2_b[i]))
        if i % 2 == 0:
            w_in, w_q, w_kv = _mla_weights(mla_w_in[j], mla_w_q_up[j], mla_w_kv_up[j])
            qn, qr, kn, kr, v, qm = _mla_proj(x.reshape(bsz * t, d), cos, sin, w_in, row(mla_q_norm[j]), w_q,
                                              row(mla_kv_norm[j]), w_kv)
            seq = lambda a: a.reshape(bsz, t, a.shape[-1])
            o_mix = _mla_attn(seq(qn), seq(qr), seq(kn), seq(kr), seq(v))
            x = _tail((o_mix,), seq(qm), mem_k, mem_v, x, *tail_w, rwkv=False, alpha=alpha)
        else:
            w_in, mu, w2, a2, g2 = _rwkv_weights(rwkv_w_in[j], rwkv_mu[j], rwkv_w2[j], rwkv_a2[j], rwkv_g2[j])
            r, lw, k, v, a, b, gate, bonus, qm = _rwkv_proj(
                x, w_in, mu, row(rwkv_w0[j]), w2, row(rwkv_a0[j]), a2, g2, row(rwkv_k_k[j]), row(rwkv_k_a[j]),
                row(rwkv_r_k[j]), seg)
            y = _rwkv_scan(r, lw, k, v, a, b)
            x = _tail((y, bonus, gate, row(rwkv_gn_g[j]), row(rwkv_gn_b[j]), seg), qm, mem_k, mem_v, x,
                      *tail_w, rwkv=True, alpha=alpha)
    return x
```

```python
import functools

import jax
import jax.numpy as jnp
from jax import lax
from jax.experimental import pallas as pl
from jax.experimental.pallas import tpu as pltpu

F32 = jnp.float32
BF16 = jnp.bfloat16

HEAD_DIM = 64
MIX_HEADS = 12
MEM_HEADS = 4
MIX_WIDTH = MIX_HEADS * HEAD_DIM
MEM_WIDTH = MEM_HEADS * HEAD_DIM
MLA_Q_RANK = 384
MLA_KV_RANK = 256
MLA_NOPE_DIM = 64
MLA_ROPE_DIM = 32
MLA_QK_DIM = MLA_NOPE_DIM + MLA_ROPE_DIM
ROPE_HALF = MLA_ROPE_DIM // 2
ROPE_BASE = 10000.0
RWKV_DECAY_RANK = 64
RWKV_A_RANK = 64
RWKV_GATE_RANK = 160
RWKV_GN_EPS = 64e-5
LN_EPS = 1e-5
RMS_EPS = 1e-6
QK_SCALE = MLA_QK_DIM ** -0.5 * 1.4426950408889634
MEM_SCALE = HEAD_DIM ** -0.5
MASK_VALUE = -1e30
DECAY_SCALE = 0.6065306597126334

LANES = 128
MXU_DIM = 256
HEADS_PER_TILE = LANES // HEAD_DIM
N_PAIRS = MIX_HEADS // HEADS_PER_TILE
VMEM_LIMIT_BYTES = 56 * 1024 * 1024

ROW_TILE = 512
ATTN_TILE = 512
ATTN_ROW_BLOCKS = 4
ATTN_TILE_PAIRS_PER_STEP = 4
SCAN_CHUNK = 64
SCAN_CHUNKS_PER_STEP = 4
SCAN_PAIRS_PER_STEP = 6
ROW_GROUPS = 2

RW_HW = 3 * MIX_WIDTH
RW_HA = RW_HW + LANES
RW_HG = RW_HA + LANES
RW_TM_COLS = RW_HG + 2 * LANES
RW_COLS = RW_TM_COLS + MEM_WIDTH

ML_CKV = MLA_Q_RANK
ML_QMEM = ML_CKV + MLA_KV_RANK
ML_KR = ML_QMEM + MEM_WIDTH
ML_KR_SWAP = ML_KR + LANES
ML_COLS = ML_KR_SWAP + LANES


def _bdot(a, b):
    return jnp.dot(a.astype(BF16), b.astype(BF16), preferred_element_type=F32)


def _bdot_nt(a, b):
    return lax.dot_general(a.astype(BF16), b.astype(BF16), (((1,), (1,)), ((), ())),
                           preferred_element_type=F32)


def _head_sums(x, seg):
    hi = x.astype(BF16)
    lo = (x - hi.astype(F32)).astype(BF16)
    parts = []
    for g in range(x.shape[1] // MXU_DIM):
        cols = slice(g * MXU_DIM, (g + 1) * MXU_DIM)
        parts.append(jnp.dot(hi[:, cols], seg, preferred_element_type=F32)
                     + jnp.dot(lo[:, cols], seg, preferred_element_type=F32))
    return jnp.concatenate(parts, axis=1)


def _layer_norm(t, g, b):
    mu = jnp.mean(t, axis=-1, keepdims=True)
    d = t - mu
    var = jnp.mean(d * d, axis=-1, keepdims=True)
    return d * lax.rsqrt(var + LN_EPS) * g + b


def _rms_norm(t, g):
    return t * lax.rsqrt(jnp.mean(t * t, axis=-1, keepdims=True) + RMS_EPS) * g


def _sigmoid(z):
    return 1.0 / (1.0 + jnp.exp(-z))


def _params(*semantics):
    return pltpu.CompilerParams(dimension_semantics=semantics, vmem_limit_bytes=VMEM_LIMIT_BYTES)


def _const_spec(shape):
    nd = len(shape)
    return pl.BlockSpec(shape, lambda *_: (0,) * nd)


def _mem_kv_kernel(mem_ref, g_ref, b_ref, w_ref, k_ref, v_ref):
    m_tokens = mem_ref.shape[1]
    kv = _bdot(_layer_norm(mem_ref[0], g_ref[...], b_ref[...]), w_ref[...])
    k = kv[:, :MEM_WIDTH]
    v = kv[:, MEM_WIDTH:]
    head_of_lane = jnp.right_shift(lax.broadcasted_iota(jnp.int32, k.shape, 1), HEAD_DIM.bit_length() - 1)
    for h in range(MEM_HEADS):
        rows = slice(h * m_tokens, (h + 1) * m_tokens)
        k_ref[0, rows, :] = jnp.where(head_of_lane == h, k, 0.0).astype(BF16)
        v_ref[0, rows, :] = jnp.where(head_of_lane == h, v, 0.0).astype(BF16)


def _mem_kv(mem, g, b, w):
    bsz, m_tokens, d = mem.shape
    out = jax.ShapeDtypeStruct((bsz, MEM_HEADS * m_tokens, MEM_WIDTH), BF16)
    blk = pl.BlockSpec((1, MEM_HEADS * m_tokens, MEM_WIDTH), lambda i: (i, 0, 0))
    return pl.pallas_call(
        _mem_kv_kernel,
        grid=(bsz,),
        in_specs=[pl.BlockSpec((1, m_tokens, d), lambda i: (i, 0, 0)),
                  _const_spec((1, d)), _const_spec((1, d)), _const_spec(w.shape)],
        out_specs=[blk, blk],
        out_shape=[out, out],
        compiler_params=_params("parallel"),
        name="mem_kv",
    )(mem, g.reshape(1, d), b.reshape(1, d), w)


def _mla_proj_kernel(x_ref, cos_ref, sin_ref, w_in_ref, qg_ref, wq_ref, kvg_ref, wkv_ref,
                     qn_ref, qr_ref, kn_ref, kr_ref, v_ref, qm_ref):
    c = _bdot(x_ref[...], w_in_ref[...])
    q = _bdot(_rms_norm(c[:, :ML_CKV], qg_ref[...]), wq_ref[...])
    kv = _bdot(_rms_norm(c[:, ML_CKV:ML_QMEM], kvg_ref[...]), wkv_ref[...])
    cos = cos_ref[...]
    sin = sin_ref[...]
    cos_all = jnp.concatenate([cos] * N_PAIRS, axis=1)
    sin_all = jnp.concatenate([sin] * N_PAIRS, axis=1)
    qn_ref[...] = (q[:, :MIX_WIDTH] * QK_SCALE).astype(BF16)
    q_rot = q[:, MIX_WIDTH:2 * MIX_WIDTH] * cos_all + q[:, 2 * MIX_WIDTH:] * sin_all
    qr_ref[...] = (q_rot * QK_SCALE).astype(BF16)
    kn_ref[...] = kv[:, :MIX_WIDTH].astype(BF16)
    v_ref[...] = kv[:, MIX_WIDTH:].astype(BF16)
    kr_ref[...] = (c[:, ML_KR:ML_KR_SWAP] * cos + c[:, ML_KR_SWAP:] * sin).astype(BF16)
    qm_ref[...] = (c[:, ML_QMEM:ML_KR] * MEM_SCALE).astype(BF16)


def _mla_proj(x2d, cos, sin, w_in, q_norm, w_q, kv_norm, w_kv):
    n, d = x2d.shape
    tm = min(ROW_TILE, n)
    row = lambda width: pl.BlockSpec((tm, width), lambda i: (i, 0))
    shape = lambda width: jax.ShapeDtypeStruct((n, width), BF16)
    return pl.pallas_call(
        _mla_proj_kernel,
        grid=(n // tm,),
        in_specs=[row(d), row(LANES), row(LANES), _const_spec(w_in.shape), _const_spec(q_norm.shape),
                  _const_spec(w_q.shape), _const_spec(kv_norm.shape), _const_spec(w_kv.shape)],
        out_specs=[row(MIX_WIDTH), row(MIX_WIDTH), row(MIX_WIDTH), row(LANES), row(MIX_WIDTH), row(MEM_WIDTH)],
        out_shape=[shape(MIX_WIDTH), shape(MIX_WIDTH), shape(MIX_WIDTH), shape(LANES), shape(MIX_WIDTH),
                   shape(MEM_WIDTH)],
        compiler_params=_params("parallel"),
        name="mla_proj",
    )(x2d, cos, sin, w_in, q_norm, w_q, kv_norm, w_kv)


def _mla_attn_kernel(qn_ref, qr_ref, kn_ref, kr_ref, v_ref, o_ref, q_sc, m_sc, acc_sc, sa_sc, sb_sc):
    tq = sa_sc.shape[1]
    n_duos = qn_ref.shape[1] // (2 * tq)
    lane = lax.broadcasted_iota(jnp.int32, (tq, LANES), 1)

    def start(slot, tile):
        qn = qn_ref[0, tile * tq:(tile + 1) * tq, :]
        qr = qr_ref[0, tile * tq:(tile + 1) * tq, :]
        zero = jnp.zeros_like(qn)
        q_sc[slot, :tq, :LANES] = jnp.where(lane < HEAD_DIM, qn, zero)
        q_sc[slot, :tq, LANES:] = jnp.where(lane < MLA_ROPE_DIM, qr, zero)
        q_sc[slot, tq:, :LANES] = jnp.where(lane >= HEAD_DIM, qn, zero)
        q_sc[slot, tq:, LANES:] = jnp.where(lane >= MLA_ROPE_DIM, qr, zero)
        m_sc[slot] = jnp.full(m_sc.shape[1:], -jnp.inf, F32)
        acc_sc[slot] = jnp.zeros(acc_sc.shape[1:], F32)

    ones = jnp.ones((tq, LANES), BF16)
    rb = HEADS_PER_TILE * tq // ATTN_ROW_BLOCKS

    def scores(slot, j, s_ref):
        rows = pl.ds(pl.multiple_of(j * tq, tq), tq)
        k = jnp.concatenate([kn_ref[0, rows, :], kr_ref[0, rows, :]], axis=1)
        s_ref[...] = lax.dot_general(q_sc[slot], k, (((1,), (1,)), ((), ())), preferred_element_type=F32)

    def consume(slot, j, s_ref, masked):
        rows = pl.ds(pl.multiple_of(j * tq, tq), tq)
        v_ones = jnp.concatenate([v_ref[0, rows, :], ones], axis=1)
        for blk in range(ATTN_ROW_BLOCKS):
            brows = slice(blk * rb, (blk + 1) * rb)
            s = s_ref[brows, :]
            if masked:
                q_pos = lax.broadcasted_iota(jnp.int32, s.shape, 0) + (blk * rb) % tq
                k_pos = lax.broadcasted_iota(jnp.int32, s.shape, 1)
                s = jnp.where(k_pos <= q_pos, s, MASK_VALUE)
            m_prev = m_sc[slot, brows, :]
            m_new = jnp.maximum(m_prev, jnp.max(s, axis=1, keepdims=True))
            alpha = jnp.exp2(m_prev - m_new)
            p = jnp.exp2(s - jnp.concatenate([m_new] * (tq // LANES), axis=1))
            acc_sc[slot, brows, :] = (jnp.concatenate([alpha, alpha], axis=1) * acc_sc[slot, brows, :]
                                      + jnp.dot(p.astype(BF16), v_ones, preferred_element_type=F32))
            m_sc[slot, brows, :] = m_new

    def finish(slot, tile):
        out = acc_sc[slot, :, :LANES] / acc_sc[slot, :, LANES:]
        o_ref[0, tile * tq:(tile + 1) * tq, :] = jnp.where(lane < HEAD_DIM, out[:tq], out[tq:]).astype(BF16)

    def two_tiles(slot, first, second):
        def body(i, carry):
            scores(slot, 2 * i + 1, second)
            consume(slot, 2 * i, first, False)
            scores(slot, 2 * i + 2, first)
            consume(slot, 2 * i + 1, second, False)
            return carry
        return body

    buf_x, buf_y = sa_sc, sb_sc
    for tile in range(2 * n_duos):
        start(tile, tile)
    scores(0, 0, buf_x)
    for duo in range(n_duos):
        mi = pl.program_id(2) * n_duos + duo
        a, b = 2 * duo, 2 * duo + 1
        lax.fori_loop(0, mi, two_tiles(a, buf_x, buf_y), 0)
        scores(b, 0, buf_y)
        consume(a, 2 * mi, buf_x, True)
        finish(a, a)
        lax.fori_loop(0, mi, two_tiles(b, buf_y, buf_x), 0)
        scores(b, 2 * mi + 1, buf_x)
        consume(b, 2 * mi, buf_y, False)
        if duo + 1 < n_duos:
            scores(b + 1, 0, buf_y)
        consume(b, 2 * mi + 1, buf_x, True)
        finish(b, b)
        buf_x, buf_y = buf_y, buf_x


def _mla_attn(qn, qr, kn, kr, v):
    bsz, t, _ = qn.shape
    tq = min(ATTN_TILE, t)
    assert tq & (tq - 1) == 0 and t % (2 * tq) == 0
    tiles = 2 * min(ATTN_TILE_PAIRS_PER_STEP, t // (2 * tq))
    assert t % (tiles * tq) == 0
    q_spec = pl.BlockSpec((1, tiles * tq, LANES), lambda b, p, i: (b, i, p))
    seq_spec = pl.BlockSpec((1, t, LANES), lambda b, p, i: (b, 0, p))
    return pl.pallas_call(
        _mla_attn_kernel,
        grid=(bsz, N_PAIRS, t // (tiles * tq)),
        in_specs=[q_spec, q_spec, seq_spec, pl.BlockSpec((1, t, LANES), lambda b, p, i: (b, 0, 0)), seq_spec],
        out_specs=q_spec,
        out_shape=jax.ShapeDtypeStruct((bsz, t, MIX_WIDTH), BF16),
        scratch_shapes=[pltpu.VMEM((tiles, HEADS_PER_TILE * tq, 2 * LANES), BF16),
                        pltpu.VMEM((tiles, HEADS_PER_TILE * tq, LANES), F32),
                        pltpu.VMEM((tiles, HEADS_PER_TILE * tq, 2 * LANES), F32),
                        pltpu.VMEM((HEADS_PER_TILE * tq, tq), F32),
                        pltpu.VMEM((HEADS_PER_TILE * tq, tq), F32)],
        compiler_params=_params("parallel", "parallel", "arbitrary"),
        name="mla_attn",
    )(qn, qr, kn, kr, v)


def _rwkv_proj_kernel(x_ref, w_in_ref, mu_ref, w0_ref, w2_ref, a0_ref, a2_ref, g2_ref, kk_ref, ka_ref,
                      rk_ref, seg_ref, r_o, lw_o, k_o, v_o, a_o, b_o, g_o, bonus_o, qm_o, carry_sc):
    n_groups = carry_sc.shape[0] // 8
    gm = x_ref.shape[1] // n_groups
    last_slot = slice(8 * (n_groups - 1), 8 * n_groups)

    @pl.when(pl.program_id(1) == 0)
    def _():
        carry_sc[last_slot, :] = jnp.zeros((8, carry_sc.shape[1]), F32)

    first_row = lax.broadcasted_iota(jnp.int32, (gm, 1), 0) == 0
    seg = seg_ref[...]
    projs = [_bdot(x_ref[0, gi * gm:(gi + 1) * gm, :], w_in_ref[...]) for gi in range(n_groups)]

    for gi, proj in enumerate(projs):
        rows = slice(gi * gm, (gi + 1) * gm)
        prev_slot = (gi - 1) % n_groups
        prev = carry_sc[8 * prev_slot + 7:8 * prev_slot + 8, :]
        carry_sc[8 * gi:8 * gi + 8, :] = proj[gm - 8:, :RW_TM_COLS]

        def mixed(lo, hi):
            cur = proj[:, lo:hi]
            shifted = jnp.where(first_row, prev[:, lo:hi], pltpu.roll(cur, 1, 0))
            return cur + (shifted - cur) * mu_ref[:, lo:hi]

        r = mixed(0, MIX_WIDTH)
        k = mixed(MIX_WIDTH, 2 * MIX_WIDTH)
        v = mixed(2 * MIX_WIDTH, RW_HW)
        hw = mixed(RW_HW, RW_HA)
        ha = mixed(RW_HA, RW_HG)
        hg = mixed(RW_HG, RW_TM_COLS)

        den = 1.0 + jnp.exp(-(w0_ref[...] + _bdot(jnp.tanh(hw), w2_ref[...])))
        rcp = 1.0 / den
        rcp = jnp.where(den < 1e30, rcp * (2.0 - den * rcp), rcp)
        log_decay = -DECAY_SCALE * rcp
        a = _sigmoid(a0_ref[...] + _bdot(ha, a2_ref[...]))
        gate = _bdot(_sigmoid(hg), g2_ref[...])

        kk = k * kk_ref[...]
        kk = kk * jnp.minimum(lax.rsqrt(_head_sums(kk * kk, seg)), 1e12)
        k2 = k * (1.0 + (a - 1.0) * ka_ref[...])
        bonus = _head_sums(r * k2 * rk_ref[...], seg) * v

        r_o[0, rows, :] = r
        lw_o[0, rows, :] = log_decay
        k_o[0, rows, :] = k2
        v_o[0, rows, :] = v.astype(BF16)
        a_o[0, rows, :] = -kk
        b_o[0, rows, :] = kk * a
        g_o[0, rows, :] = gate
        bonus_o[0, rows, :] = bonus
        qm_o[0, rows, :] = (proj[:, RW_TM_COLS:] * MEM_SCALE).astype(BF16)


def _rwkv_proj(x, w_in, mu, w0, w2, a0, a2, g2, k_k, k_a, r_k, seg):
    bsz, t, d = x.shape
    tm = min(ROW_TILE, t)
    row = lambda width: pl.BlockSpec((1, tm, width), lambda b, i: (b, i, 0))
    wide = jax.ShapeDtypeStruct((bsz, t, MIX_WIDTH), F32)
    consts = (w_in, mu, w0, w2, a0, a2, g2, k_k, k_a, r_k, seg)
    return pl.pallas_call(
        _rwkv_proj_kernel,
        grid=(bsz, t // tm),
        in_specs=[row(d)] + [_const_spec(c.shape) for c in consts],
        out_specs=[row(MIX_WIDTH)] * 8 + [row(MEM_WIDTH)],
        out_shape=[wide] * 3 + [jax.ShapeDtypeStruct((bsz, t, MIX_WIDTH), BF16)] + [wide] * 4
        + [jax.ShapeDtypeStruct((bsz, t, MEM_WIDTH), BF16)],
        scratch_shapes=[pltpu.VMEM((8 * ROW_GROUPS, RW_TM_COLS), F32)],
        compiler_params=_params("parallel", "arbitrary"),
        name="rwkv_proj",
    )(x, *consts)


def _rwkv_scan_kernel(r_ref, lw_ref, k_ref, v_ref, a_ref, b_ref, y_ref, state_sc):
    c = SCAN_CHUNK
    n = HEADS_PER_TILE * c
    n_pairs = r_ref.shape[2] // LANES
    items = [(p, cj) for cj in range(r_ref.shape[1] // c) for p in range(n_pairs)]
    chunks = range(len(items))

    @pl.when(pl.program_id(2) == 0)
    def _():
        state_sc[...] = jnp.zeros(state_sc.shape, F32)

    head0 = lax.broadcasted_iota(jnp.int32, (c, LANES), 1) < HEAD_DIM

    def stack(z):
        return jnp.concatenate([jnp.where(head0, z, 0.0), jnp.where(head0, 0.0, z)], axis=0)

    def stack_bf16(z):
        return stack(z).astype(BF16)

    tril_ones = jnp.where(lax.broadcasted_iota(jnp.int32, (c, c), 0)
                          >= lax.broadcasted_iota(jnp.int32, (c, c), 1), 1.0, 0.0).astype(BF16)
    t_row = lax.broadcasted_iota(jnp.int32, (c, LANES), 0)
    s_col = jnp.bitwise_and(lax.broadcasted_iota(jnp.int32, (c, LANES), 1), HEAD_DIM - 1)
    strict = t_row > s_col
    incl = t_row >= s_col

    rt, at, bk, vs, w_end = [], [], [], [], []
    for p, cj in items:
        rows = slice(cj * c, (cj + 1) * c)
        lanes = slice(p * LANES, (p + 1) * LANES)
        lw = lw_ref[0, rows, lanes]
        lw_hi = lw.astype(BF16)
        lw_mid = (lw - lw_hi.astype(F32)).astype(BF16)
        lw_lo = (lw - lw_hi.astype(F32) - lw_mid.astype(F32)).astype(BF16)
        cum = (jnp.dot(tril_ones, lw_hi, preferred_element_type=F32)
               + jnp.dot(tril_ones, lw_mid, preferred_element_type=F32)
               + jnp.dot(tril_ones, lw_lo, preferred_element_type=F32))
        w_incl = jnp.exp(cum)
        w_inv = jnp.exp(-cum)
        rt.append(r_ref[0, rows, lanes] * w_incl)
        at.append(a_ref[0, rows, lanes] * jnp.exp(cum - lw))
        bk.append(jnp.concatenate([stack(b_ref[0, rows, lanes] * w_inv), stack(k_ref[0, rows, lanes] * w_inv)],
                                  axis=0).astype(BF16))
        vs.append(stack(v_ref[0, rows, lanes]))
        w_end.append(w_incl[c - 1:c, :])

    m = [_bdot_nt(jnp.concatenate([at[ci], rt[ci]], axis=0), bk[ci]) for ci in chunks]
    a_pow = [jnp.where(strict, m[ci][:c, :n], 0.0) for ci in chunks]
    inv_m1 = a_pow
    reach = 2
    while reach < c:
        pow_s = [stack_bf16(a_pow[ci]) for ci in chunks]
        a_pow = [_bdot(a_pow[ci], pow_s[ci]) for ci in chunks]
        pow_s = [stack_bf16(a_pow[ci]) for ci in chunks]
        inv_m1 = [inv_m1[ci] + a_pow[ci] + _bdot(inv_m1[ci], pow_s[ci]) for ci in chunks]
        reach *= 2
    akv = [_bdot(jnp.where(strict, m[ci][:c, n:], 0.0), vs[ci]) for ci in chunks]
    x = [jnp.concatenate([at[ci], akv[ci]], axis=1)
         + _bdot(inv_m1[ci], jnp.concatenate([stack(at[ci]), stack(akv[ci])], axis=1))
         for ci in chunks]
    xs = [jnp.concatenate([stack(x[ci][:, :LANES]), stack(x[ci][:, LANES:])], axis=1)
          for ci in chunks]
    z = [_bdot(jnp.where(incl, m[ci][c:, :n], 0.0), xs[ci]) for ci in chunks]
    r_eff = [rt[ci] + z[ci][:, :LANES] for ci in chunks]
    y_loc = [z[ci][:, LANES:] + _bdot(jnp.where(incl, m[ci][c:, n:], 0.0), vs[ci]) for ci in chunks]
    g = [_bdot(xs[ci][:, :LANES].T, bk[ci][:n]) for ci in chunks]
    h = [_bdot(jnp.concatenate([xs[ci][:, LANES:], vs[ci]], axis=0).T, bk[ci]) for ci in chunks]

    states = [state_sc[p] for p in range(n_pairs)]
    for ci, (p, cj) in enumerate(items):
        y_ref[0, cj * c:(cj + 1) * c, p * LANES:(p + 1) * LANES] = _bdot_nt(r_eff[ci], states[p]) + y_loc[ci]
        states[p] = (states[p] + _bdot(states[p], g[ci]) + h[ci]) * w_end[ci]
    for p in range(n_pairs):
        state_sc[p] = states[p]


def _rwkv_scan(r, lw, k, v, a, b):
    bsz, t, _ = r.shape
    tc = min(SCAN_CHUNK * SCAN_CHUNKS_PER_STEP, t)
    spec = pl.BlockSpec((1, tc, SCAN_PAIRS_PER_STEP * LANES), lambda bi, p, c: (bi, c, p))
    return pl.pallas_call(
        _rwkv_scan_kernel,
        grid=(bsz, N_PAIRS // SCAN_PAIRS_PER_STEP, t // tc),
        in_specs=[spec] * 6,
        out_specs=spec,
        out_shape=jax.ShapeDtypeStruct((bsz, t, MIX_WIDTH), F32),
        scratch_shapes=[pltpu.VMEM((SCAN_PAIRS_PER_STEP, LANES, LANES), F32)],
        compiler_params=_params("parallel", "parallel", "arbitrary"),
        name="rwkv_scan",
    )(r, lw, k, v, a, b)


def _tail_kernel(*refs, rwkv, alpha):
    if rwkv:
        y_ref, bonus_ref, gate_ref, gng_ref, gnb_ref, seg_ref = refs[:6]
        refs = refs[6:]
    else:
        y_ref = refs[0]
        refs = refs[1:]
    (qm_ref, mk_ref, mv_ref, x_ref, wo_mix_ref, wo_mem_ref, ln1g_ref, ln1b_ref,
     w1_ref, w2_ref, ln2g_ref, ln2b_ref, o_ref) = refs

    m_tokens = mk_ref.shape[1] // MEM_HEADS
    tm = x_ref.shape[1]
    groups = [slice(i * (tm // ROW_GROUPS), (i + 1) * (tm // ROW_GROUPS)) for i in range(ROW_GROUPS)]

    def mixed_and_normed(rows):
        if rwkv:
            y = y_ref[0, rows, :]
            seg = seg_ref[...]
            mean = _head_sums(y, seg) * (1.0 / HEAD_DIM)
            d = y - mean
            var = _head_sums(d * d, seg) * (1.0 / HEAD_DIM)
            yn = d * lax.rsqrt(var + RWKV_GN_EPS) * gng_ref[...] + gnb_ref[...]
            o_mix = ((yn + bonus_ref[0, rows, :]) * gate_ref[0, rows, :]).astype(BF16)
        else:
            o_mix = y_ref[0, rows, :]
        s = lax.dot_general(qm_ref[0, rows, :], mk_ref[0], (((1,), (1,)), ((), ())),
                            preferred_element_type=F32)
        probs = []
        for h in range(MEM_HEADS):
            sh = s[:, h * m_tokens:(h + 1) * m_tokens]
            e = jnp.exp(sh - jnp.max(sh, axis=1, keepdims=True))
            probs.append((e / jnp.sum(e, axis=1, keepdims=True)).astype(BF16))
        o_mem = jnp.dot(jnp.concatenate(probs, axis=1), mv_ref[0], preferred_element_type=F32)
        mix = (jnp.dot(o_mix, wo_mix_ref[...], preferred_element_type=F32)
               + _bdot(o_mem, wo_mem_ref[...]))
        return _layer_norm(alpha * x_ref[0, rows, :] + mix, ln1g_ref[...], ln1b_ref[...])

    def mlp(x1):
        hidden = jnp.dot(x1.astype(BF16), w1_ref[...], preferred_element_type=F32)
        hidden = jnp.square(jnp.maximum(hidden, 0.0)).astype(BF16)
        ff = jnp.dot(hidden, w2_ref[...], preferred_element_type=F32)
        return _layer_norm(alpha * x1 + ff, ln2g_ref[...], ln2b_ref[...])

    x1 = [mixed_and_normed(rows) for rows in groups]
    for rows, x1_rows in zip(groups, x1):
        o_ref[0, rows, :] = mlp(x1_rows)


def _tail(mix_inputs, qm, mem_k, mem_v, x, wo_mix, wo_mem, ln1g, ln1b, w1, w2, ln2g, ln2b, *, rwkv, alpha):
    bsz, t, d = x.shape
    tm = min(ROW_TILE, t)
    row = lambda width: pl.BlockSpec((1, tm, width), lambda b, i: (b, i, 0))
    mem_spec = pl.BlockSpec((1,) + mem_k.shape[1:], lambda b, i: (b, 0, 0))
    single = lambda a: pl.BlockSpec(a.shape, lambda b, i: (0,) * a.ndim, pipeline_mode=pl.Buffered(1))
    if rwkv:
        y, bonus, gate, gng, gnb, seg = mix_inputs
        head_args = [y, bonus, gate, gng, gnb, seg]
        head_specs = [row(MIX_WIDTH)] * 3 + [single(gng), single(gnb), single(seg)]
    else:
        head_args = list(mix_inputs)
        head_specs = [row(MIX_WIDTH)]
    consts = (wo_mix, wo_mem, ln1g, ln1b, w1, w2, ln2g, ln2b)
    return pl.pallas_call(
        functools.partial(_tail_kernel, rwkv=rwkv, alpha=alpha),
        grid=(bsz, t // tm),
        in_specs=head_specs + [row(MEM_WIDTH), mem_spec, mem_spec, row(d)] + [single(c) for c in consts],
        out_specs=row(d),
        out_shape=jax.ShapeDtypeStruct((bsz, t, d), F32),
        compiler_params=_params("parallel", "parallel"),
        name="rwkv_tail" if rwkv else "mla_tail",
    )(*head_args, qm, mem_k, mem_v, x, *consts)


def _swap_halves(w):
    half = w.shape[-1] // 2
    return jnp.concatenate([w[..., half:], w[..., :half]], axis=-1)


def _pair_pack(w):
    kdim = w.shape[0]
    w = w.reshape(kdim, N_PAIRS, HEADS_PER_TILE * MLA_ROPE_DIM)
    pad = jnp.zeros((kdim, N_PAIRS, LANES - HEADS_PER_TILE * MLA_ROPE_DIM), w.dtype)
    return jnp.concatenate([w, pad], axis=-1).reshape(kdim, N_PAIRS * LANES)


def _mla_weights(w_in, w_q_up, w_kv_up):
    d = w_in.shape[0]
    o_kv, o_kr, o_qm = MLA_Q_RANK, MLA_Q_RANK + MLA_KV_RANK, MLA_Q_RANK + MLA_KV_RANK + MLA_ROPE_DIM
    w_kr = w_in[:, o_kr:o_qm]
    pad = jnp.zeros((d, LANES - 2 * MLA_ROPE_DIM), w_in.dtype)
    w_in_p = jnp.concatenate([w_in[:, :o_kr], w_in[:, o_qm:], w_kr, w_kr, pad,
                              _swap_halves(w_kr), _swap_halves(w_kr), pad], axis=1)
    wq = w_q_up.reshape(MLA_Q_RANK, MIX_HEADS, MLA_QK_DIM)
    rope = wq[:, :, MLA_NOPE_DIM:]
    w_q = jnp.concatenate([wq[:, :, :MLA_NOPE_DIM].reshape(MLA_Q_RANK, MIX_WIDTH),
                           _pair_pack(rope), _pair_pack(_swap_halves(rope))], axis=1)
    wkv = w_kv_up.reshape(MLA_KV_RANK, MIX_HEADS, 2 * HEAD_DIM)
    w_kv = jnp.concatenate([wkv[:, :, :HEAD_DIM].reshape(MLA_KV_RANK, MIX_WIDTH),
                            wkv[:, :, HEAD_DIM:].reshape(MLA_KV_RANK, MIX_WIDTH)], axis=1)
    return w_in_p.astype(BF16), w_q.astype(BF16), w_kv.astype(BF16)


def _rope_tables(positions):
    lane = jnp.arange(LANES)
    inv_freq = ROPE_BASE ** (-(lane % ROPE_HALF).astype(F32) * 2.0 / MLA_ROPE_DIM)
    ang = positions.astype(F32).reshape(-1, 1) * inv_freq
    used = lane < HEADS_PER_TILE * MLA_ROPE_DIM
    sign = jnp.where((lane // ROPE_HALF) % 2 == 0, -1.0, 1.0).astype(F32)
    return jnp.where(used, jnp.cos(ang), 0.0), jnp.where(used, sign * jnp.sin(ang), 0.0)


def _pad_cols(w, width):
    return jnp.concatenate([w, jnp.zeros(w.shape[:-1] + (width - w.shape[-1],), w.dtype)], axis=-1)


def _pad_rows(w, height):
    return jnp.concatenate([w, jnp.zeros((height - w.shape[0],) + w.shape[1:], w.dtype)], axis=0)


def _rwkv_weights(w_in, mu, w2, a2, g2):
    o_hw, o_ha = 3 * MIX_WIDTH, 3 * MIX_WIDTH + RWKV_DECAY_RANK
    o_hg = o_ha + RWKV_A_RANK
    o_qm = o_hg + RWKV_GATE_RANK

    def layout(w):
        return jnp.concatenate([w[..., :o_hw], _pad_cols(w[..., o_hw:o_ha], LANES),
                                _pad_cols(w[..., o_ha:o_hg], LANES),
                                _pad_cols(w[..., o_hg:o_qm], 2 * LANES)], axis=-1)

    w_in_p = jnp.concatenate([layout(w_in[:, :o_qm]), w_in[:, o_qm:]], axis=1)
    return (w_in_p.astype(BF16), layout(mu).reshape(1, RW_TM_COLS),
            _pad_rows(w2, LANES).astype(BF16), _pad_rows(a2, LANES).astype(BF16),
            _pad_rows(g2, 2 * LANES).astype(BF16))


def kernel(x, mem, positions, mem_ln_g, mem_ln_b, w_mem_kv, mla_w_in, mla_q_norm, mla_w_q_up, mla_kv_norm, mla_w_kv_up, rwkv_w_in, rwkv_mu, rwkv_w0, rwkv_w2, rwkv_a0, rwkv_a2, rwkv_g2, rwkv_k_k, rwkv_k_a, rwkv_r_k, rwkv_gn_g, rwkv_gn_b, w_out, ln1_g, ln1_b, w_ff1, w_ff2, ln2_g, ln2_b):
    bsz, t, d = x.shape
    depth = w_out.shape[0]
    alpha = (2.0 * depth) ** 0.25
    row = lambda p: p.reshape(1, -1)

    mem_k, mem_v = _mem_kv(mem, mem_ln_g, mem_ln_b, w_mem_kv.astype(BF16))
    cos, sin = _rope_tables(positions)
    head_of = jnp.arange(MXU_DIM) // HEAD_DIM
    seg = (head_of[:, None] == head_of[None, :]).astype(BF16)

    for i in range(depth):
        j = i // 2
        tail_w = (w_out[i, :MIX_WIDTH].astype(BF16), w_out[i, MIX_WIDTH:].astype(BF16),
                  row(ln1_g[i]), row(ln1_b[i]), w_ff1[i].astype(BF16), w_ff2[i].astype(BF16),
                  row(ln2_g[i]), row(ln2_b[i]))
        if i % 2 == 0:
            w_in, w_q, w_kv = _mla_weights(mla_w_in[j], mla_w_q_up[j], mla_w_kv_up[j])
            qn, qr, kn, kr, v, qm = _mla_proj(x.reshape(bsz * t, d), cos, sin, w_in, row(mla_q_norm[j]), w_q,
                                              row(mla_kv_norm[j]), w_kv)
            seq = lambda a: a.reshape(bsz, t, a.shape[-1])
            o_mix = _mla_attn(seq(qn), seq(qr), seq(kn), seq(kr), seq(v))
            x = _tail((o_mix,), seq(qm), mem_k, mem_v, x, *tail_w, rwkv=False, alpha=alpha)
        else:
            w_in, mu, w2, a2, g2 = _rwkv_weights(rwkv_w_in[j], rwkv_mu[j], rwkv_w2[j], rwkv_a2[j], rwkv_g2[j])
            r, lw, k, v, a, b, gate, bonus, qm = _rwkv_proj(
                x, w_in, mu, row(rwkv_w0[j]), w2, row(rwkv_a0[j]), a2, g2, row(rwkv_k_k[j]), row(rwkv_k_a[j]),
                row(rwkv_r_k[j]), seg)
            y = _rwkv_scan(r, lw, k, v, a, b)
            x = _tail((y, bonus, gate, row(rwkv_gn_g[j]), row(rwkv_gn_b[j]), seg), qm, mem_k, mem_v, x,
                      *tail_w, rwkv=True, alpha=alpha)
    return x
```

```python
import functools

import jax
import jax.numpy as jnp
from jax import lax
from jax.experimental import pallas as pl
from jax.experimental.pallas import tpu as pltpu

F32 = jnp.float32
BF16 = jnp.bfloat16

HEAD_DIM = 64
MIX_HEADS = 12
MEM_HEADS = 4
MIX_WIDTH = MIX_HEADS * HEAD_DIM
MEM_WIDTH = MEM_HEADS * HEAD_DIM
MLA_Q_RANK = 384
MLA_KV_RANK = 256
MLA_NOPE_DIM = 64
MLA_ROPE_DIM = 32
MLA_QK_DIM = MLA_NOPE_DIM + MLA_ROPE_DIM
ROPE_HALF = MLA_ROPE_DIM // 2
ROPE_BASE = 10000.0
RWKV_DECAY_RANK = 64
RWKV_A_RANK = 64
RWKV_GATE_RANK = 160
RWKV_GN_EPS = 64e-5
LN_EPS = 1e-5
RMS_EPS = 1e-6
QK_SCALE = MLA_QK_DIM ** -0.5 * 1.4426950408889634
MEM_SCALE = HEAD_DIM ** -0.5
MASK_VALUE = -1e30
DECAY_SCALE = 0.6065306597126334

LANES = 128
MXU_DIM = 256
HEADS_PER_TILE = LANES // HEAD_DIM
N_PAIRS = MIX_HEADS // HEADS_PER_TILE
VMEM_LIMIT_BYTES = 56 * 1024 * 1024

ROW_TILE = 512
ATTN_TILE = 512
ATTN_ROW_BLOCKS = 4
ATTN_TILE_PAIRS_PER_STEP = 4
SCAN_CHUNK = 64
SCAN_CHUNKS_PER_STEP = 4
SCAN_PAIRS_PER_STEP = 6
ROW_GROUPS = 2

RW_HW = 3 * MIX_WIDTH
RW_HA = RW_HW + LANES
RW_HG = RW_HA + LANES
RW_TM_COLS = RW_HG + 2 * LANES
RW_COLS = RW_TM_COLS + MEM_WIDTH

ML_CKV = MLA_Q_RANK
ML_QMEM = ML_CKV + MLA_KV_RANK
ML_KR = ML_QMEM + MEM_WIDTH
ML_KR_SWAP = ML_KR + LANES
ML_COLS = ML_KR_SWAP + LANES


def _bdot(a, b):
    return jnp.dot(a.astype(BF16), b.astype(BF16), preferred_element_type=F32)


def _bdot_nt(a, b):
    return lax.dot_general(a.astype(BF16), b.astype(BF16), (((1,), (1,)), ((), ())),
                           preferred_element_type=F32)


def _head_sums(x, seg):
    hi = x.astype(BF16)
    lo = (x - hi.astype(F32)).astype(BF16)
    parts = []
    for g in range(x.shape[1] // MXU_DIM):
        cols = slice(g * MXU_DIM, (g + 1) * MXU_DIM)
        parts.append(jnp.dot(hi[:, cols], seg, preferred_element_type=F32)
                     + jnp.dot(lo[:, cols], seg, preferred_element_type=F32))
    return jnp.concatenate(parts, axis=1)


def _layer_norm(t, g, b):
    mu = jnp.mean(t, axis=-1, keepdims=True)
    d = t - mu
    var = jnp.mean(d * d, axis=-1, keepdims=True)
    return d * lax.rsqrt(var + LN_EPS) * g + b


def _rms_norm(t, g):
    return t * lax.rsqrt(jnp.mean(t * t, axis=-1, keepdims=True) + RMS_EPS) * g


def _sigmoid(z):
    return 1.0 / (1.0 + jnp.exp(-z))


def _params(*semantics):
    return pltpu.CompilerParams(dimension_semantics=semantics, vmem_limit_bytes=VMEM_LIMIT_BYTES)


def _const_spec(shape):
    nd = len(shape)
    return pl.BlockSpec(shape, lambda *_: (0,) * nd)


def _mem_kv_kernel(mem_ref, g_ref, b_ref, w_ref, k_ref, v_ref):
    m_tokens = mem_ref.shape[1]
    kv = _bdot(_layer_norm(mem_ref[0], g_ref[...], b_ref[...]), w_ref[...])
    k = kv[:, :MEM_WIDTH]
    v = kv[:, MEM_WIDTH:]
    head_of_lane = jnp.right_shift(lax.broadcasted_iota(jnp.int32, k.shape, 1), HEAD_DIM.bit_length() - 1)
    for h in range(MEM_HEADS):
        rows = slice(h * m_tokens, (h + 1) * m_tokens)
        k_ref[0, rows, :] = jnp.where(head_of_lane == h, k, 0.0).astype(BF16)
        v_ref[0, rows, :] = jnp.where(head_of_lane == h, v, 0.0).astype(BF16)


def _mem_kv(mem, g, b, w):
    bsz, m_tokens, d = mem.shape
    out = jax.ShapeDtypeStruct((bsz, MEM_HEADS * m_tokens, MEM_WIDTH), BF16)
    blk = pl.BlockSpec((1, MEM_HEADS * m_tokens, MEM_WIDTH), lambda i: (i, 0, 0))
    return pl.pallas_call(
        _mem_kv_kernel,
        grid=(bsz,),
        in_specs=[pl.BlockSpec((1, m_tokens, d), lambda i: (i, 0, 0)),
                  _const_spec((1, d)), _const_spec((1, d)), _const_spec(w.shape)],
        out_specs=[blk, blk],
        out_shape=[out, out],
        compiler_params=_params("parallel"),
        name="mem_kv",
    )(mem, g.reshape(1, d), b.reshape(1, d), w)


def _mla_proj_kernel(x_ref, cos_ref, sin_ref, w_in_ref, qg_ref, wq_ref, kvg_ref, wkv_ref,
                     qn_ref, qr_ref, kn_ref, kr_ref, v_ref, qm_ref):
    c = _bdot(x_ref[...], w_in_ref[...])
    q = _bdot(_rms_norm(c[:, :ML_CKV], qg_ref[...]), wq_ref[...])
    kv = _bdot(_rms_norm(c[:, ML_CKV:ML_QMEM], kvg_ref[...]), wkv_ref[...])
    cos = cos_ref[...]
    sin = sin_ref[...]
    cos_all = jnp.concatenate([cos] * N_PAIRS, axis=1)
    sin_all = jnp.concatenate([sin] * N_PAIRS, axis=1)
    qn_ref[...] = (q[:, :MIX_WIDTH] * QK_SCALE).astype(BF16)
    q_rot = q[:, MIX_WIDTH:2 * MIX_WIDTH] * cos_all + q[:, 2 * MIX_WIDTH:] * sin_all
    qr_ref[...] = (q_rot * QK_SCALE).astype(BF16)
    kn_ref[...] = kv[:, :MIX_WIDTH].astype(BF16)
    v_ref[...] = kv[:, MIX_WIDTH:].astype(BF16)
    kr_ref[...] = (c[:, ML_KR:ML_KR_SWAP] * cos + c[:, ML_KR_SWAP:] * sin).astype(BF16)
    qm_ref[...] = (c[:, ML_QMEM:ML_KR] * MEM_SCALE).astype(BF16)


def _mla_proj(x2d, cos, sin, w_in, q_norm, w_q, kv_norm, w_kv):
    n, d = x2d.shape
    tm = min(ROW_TILE, n)
    row = lambda width: pl.BlockSpec((tm, width), lambda i: (i, 0))
    shape = lambda width: jax.ShapeDtypeStruct((n, width), BF16)
    return pl.pallas_call(
        _mla_proj_kernel,
        grid=(n // tm,),
        in_specs=[row(d), row(LANES), row(LANES), _const_spec(w_in.shape), _const_spec(q_norm.shape),
                  _const_spec(w_q.shape), _const_spec(kv_norm.shape), _const_spec(w_kv.shape)],
        out_specs=[row(MIX_WIDTH), row(MIX_WIDTH), row(MIX_WIDTH), row(LANES), row(MIX_WIDTH), row(MEM_WIDTH)],
        out_shape=[shape(MIX_WIDTH), shape(MIX_WIDTH), shape(MIX_WIDTH), shape(LANES), shape(MIX_WIDTH),
                   shape(MEM_WIDTH)],
        compiler_params=_params("parallel"),
        name="mla_proj",
    )(x2d, cos, sin, w_in, q_norm, w_q, kv_norm, w_kv)


def _mla_attn_kernel(qn_ref, qr_ref, kn_ref, kr_ref, v_ref, o_ref, q_sc, m_sc, acc_sc, sa_sc, sb_sc):
    tq = sa_sc.shape[1]
    n_duos = qn_ref.shape[1] // (2 * tq)
    lane = lax.broadcasted_iota(jnp.int32, (tq, LANES), 1)

    def start(slot, tile):
        qn = qn_ref[0, tile * tq:(tile + 1) * tq, :]
        qr = qr_ref[0, tile * tq:(tile + 1) * tq, :]
        zero = jnp.zeros_like(qn)
        q_sc[slot, :tq, :LANES] = jnp.where(lane < HEAD_DIM, qn, zero)
        q_sc[slot, :tq, LANES:] = jnp.where(lane < MLA_ROPE_DIM, qr, zero)
        q_sc[slot, tq:, :LANES] = jnp.where(lane >= HEAD_DIM, qn, zero)
        q_sc[slot, tq:, LANES:] = jnp.where(lane >= MLA_ROPE_DIM, qr, zero)
        m_sc[slot] = jnp.full(m_sc.shape[1:], -jnp.inf, F32)
        acc_sc[slot] = jnp.zeros(acc_sc.shape[1:], F32)

    ones = jnp.ones((tq, LANES), BF16)
    rb = HEADS_PER_TILE * tq // ATTN_ROW_BLOCKS

    def scores(slot, j, s_ref):
        rows = pl.ds(pl.multiple_of(j * tq, tq), tq)
        k = jnp.concatenate([kn_ref[0, rows, :], kr_ref[0, rows, :]], axis=1)
        s_ref[...] = lax.dot_general(q_sc[slot], k, (((1,), (1,)), ((), ())), preferred_element_type=F32)

    def consume(slot, j, s_ref, masked):
        rows = pl.ds(pl.multiple_of(j * tq, tq), tq)
        v_ones = jnp.concatenate([v_ref[0, rows, :], ones], axis=1)
        for blk in range(ATTN_ROW_BLOCKS):
            brows = slice(blk * rb, (blk + 1) * rb)
            s = s_ref[brows, :]
            if masked:
                q_pos = lax.broadcasted_iota(jnp.int32, s.shape, 0) + (blk * rb) % tq
                k_pos = lax.broadcasted_iota(jnp.int32, s.shape, 1)
                s = jnp.where(k_pos <= q_pos, s, MASK_VALUE)
            m_prev = m_sc[slot, brows, :]
            m_new = jnp.maximum(m_prev, jnp.max(s, axis=1, keepdims=True))
            alpha = jnp.exp2(m_prev - m_new)
            p = jnp.exp2(s - jnp.concatenate([m_new] * (tq // LANES), axis=1))
            acc_sc[slot, brows, :] = (jnp.concatenate([alpha, alpha], axis=1) * acc_sc[slot, brows, :]
                                      + jnp.dot(p.astype(BF16), v_ones, preferred_element_type=F32))
            m_sc[slot, brows, :] = m_new

    def finish(slot, tile):
        out = acc_sc[slot, :, :LANES] / acc_sc[slot, :, LANES:]
        o_ref[0, tile * tq:(tile + 1) * tq, :] = jnp.where(lane < HEAD_DIM, out[:tq], out[tq:]).astype(BF16)

    def two_tiles(slot, first, second):
        def body(i, carry):
            scores(slot, 2 * i + 1, second)
            consume(slot, 2 * i, first, False)
            scores(slot, 2 * i + 2, first)
            consume(slot, 2 * i + 1, second, False)
            return carry
        return body

    buf_x, buf_y = sa_sc, sb_sc
    for tile in range(2 * n_duos):
        start(tile, tile)
    scores(0, 0, buf_x)
    for duo in range(n_duos):
        mi = pl.program_id(2) * n_duos + duo
        a, b = 2 * duo, 2 * duo + 1
        lax.fori_loop(0, mi, two_tiles(a, buf_x, buf_y), 0)
        scores(b, 0, buf_y)
        consume(a, 2 * mi, buf_x, True)
        finish(a, a)
        lax.fori_loop(0, mi, two_tiles(b, buf_y, buf_x), 0)
        scores(b, 2 * mi + 1, buf_x)
        consume(b, 2 * mi, buf_y, False)
        if duo + 1 < n_duos:
            scores(b + 1, 0, buf_y)
        consume(b, 2 * mi + 1, buf_x, True)
        finish(b, b)
        buf_x, buf_y = buf_y, buf_x


def _mla_attn(qn, qr, kn, kr, v):
    bsz, t, _ = qn.shape
    tq = min(ATTN_TILE, t)
    assert tq & (tq - 1) == 0 and t % (2 * tq) == 0
    tiles = 2 * min(ATTN_TILE_PAIRS_PER_STEP, t // (2 * tq))
    assert t % (tiles * tq) == 0
    q_spec = pl.BlockSpec((1, tiles * tq, LANES), lambda b, p, i: (b, i, p))
    seq_spec = pl.BlockSpec((1, t, LANES), lambda b, p, i: (b, 0, p))
    return pl.pallas_call(
        _mla_attn_kernel,
        grid=(bsz, N_PAIRS, t // (tiles * tq)),
        in_specs=[q_spec, q_spec, seq_spec, pl.BlockSpec((1, t, LANES), lambda b, p, i: (b, 0, 0)), seq_spec],
        out_specs=q_spec,
        out_shape=jax.ShapeDtypeStruct((bsz, t, MIX_WIDTH), BF16),
        scratch_shapes=[pltpu.VMEM((tiles, HEADS_PER_TILE * tq, 2 * LANES), BF16),
                        pltpu.VMEM((tiles, HEADS_PER_TILE * tq, LANES), F32),
                        pltpu.VMEM((tiles, HEADS_PER_TILE * tq, 2 * LANES), F32),
                        pltpu.VMEM((HEADS_PER_TILE * tq, tq), F32),
                        pltpu.VMEM((HEADS_PER_TILE * tq, tq), F32)],
        compiler_params=_params("parallel", "parallel", "arbitrary"),
        name="mla_attn",
    )(qn, qr, kn, kr, v)


def _rwkv_proj_kernel(x_ref, w_in_ref, mu_ref, w0_ref, w2_ref, a0_ref, a2_ref, g2_ref, kk_ref, ka_ref,
                      rk_ref, seg_ref, r_o, lw_o, k_o, v_o, a_o, b_o, g_o, bonus_o, qm_o, carry_sc):
    n_groups = carry_sc.shape[0] // 8
    gm = x_ref.shape[1] // n_groups
    last_slot = slice(8 * (n_groups - 1), 8 * n_groups)

    @pl.when(pl.program_id(1) == 0)
    def _():
        carry_sc[last_slot, :] = jnp.zeros((8, carry_sc.shape[1]), F32)

    first_row = lax.broadcasted_iota(jnp.int32, (gm, 1), 0) == 0
    seg = seg_ref[...]
    projs = [_bdot(x_ref[0, gi * gm:(gi + 1) * gm, :], w_in_ref[...]) for gi in range(n_groups)]

    for gi, proj in enumerate(projs):
        rows = slice(gi * gm, (gi + 1) * gm)
        prev_slot = (gi - 1) % n_groups
        prev = carry_sc[8 * prev_slot + 7:8 * prev_slot + 8, :]
        carry_sc[8 * gi:8 * gi + 8, :] = proj[gm - 8:, :RW_TM_COLS]

        def mixed(lo, hi):
            cur = proj[:, lo:hi]
            shifted = jnp.where(first_row, prev[:, lo:hi], pltpu.roll(cur, 1, 0))
            return cur + (shifted - cur) * mu_ref[:, lo:hi]

        r = mixed(0, MIX_WIDTH)
        k = mixed(MIX_WIDTH, 2 * MIX_WIDTH)
        v = mixed(2 * MIX_WIDTH, RW_HW)
        hw = mixed(RW_HW, RW_HA)
        ha = mixed(RW_HA, RW_HG)
        hg = mixed(RW_HG, RW_TM_COLS)

        den = 1.0 + jnp.exp(-(w0_ref[...] + _bdot(jnp.tanh(hw), w2_ref[...])))
        rcp = 1.0 / den
        rcp = jnp.where(den < 1e30, rcp * (2.0 - den * rcp), rcp)
        log_decay = -DECAY_SCALE * rcp
        a = _sigmoid(a0_ref[...] + _bdot(ha, a2_ref[...]))
        gate = _bdot(_sigmoid(hg), g2_ref[...])

        kk = k * kk_ref[...]
        kk = kk * jnp.minimum(lax.rsqrt(_head_sums(kk * kk, seg)), 1e12)
        k2 = k * (1.0 + (a - 1.0) * ka_ref[...])
        bonus = _head_sums(r * k2 * rk_ref[...], seg) * v

        r_o[0, rows, :] = r.astype(BF16)
        lw_o[0, rows, :] = log_decay
        k_o[0, rows, :] = k2.astype(BF16)
        v_o[0, rows, :] = v.astype(BF16)
        a_o[0, rows, :] = (-kk).astype(BF16)
        b_o[0, rows, :] = (kk * a).astype(BF16)
        g_o[0, rows, :] = gate
        bonus_o[0, rows, :] = bonus
        qm_o[0, rows, :] = (proj[:, RW_TM_COLS:] * MEM_SCALE).astype(BF16)


def _rwkv_proj(x, w_in, mu, w0, w2, a0, a2, g2, k_k, k_a, r_k, seg):
    bsz, t, d = x.shape
    tm = min(ROW_TILE, t)
    row = lambda width: pl.BlockSpec((1, tm, width), lambda b, i: (b, i, 0))
    wide = jax.ShapeDtypeStruct((bsz, t, MIX_WIDTH), F32)
    half = jax.ShapeDtypeStruct((bsz, t, MIX_WIDTH), BF16)
    consts = (w_in, mu, w0, w2, a0, a2, g2, k_k, k_a, r_k, seg)
    return pl.pallas_call(
        _rwkv_proj_kernel,
        grid=(bsz, t // tm),
        in_specs=[row(d)] + [_const_spec(c.shape) for c in consts],
        out_specs=[row(MIX_WIDTH)] * 8 + [row(MEM_WIDTH)],
        out_shape=[half, wide, half, half, half, half, wide, wide,
                   jax.ShapeDtypeStruct((bsz, t, MEM_WIDTH), BF16)],
        scratch_shapes=[pltpu.VMEM((8 * ROW_GROUPS, RW_TM_COLS), F32)],
        compiler_params=_params("parallel", "arbitrary"),
        name="rwkv_proj",
    )(x, *consts)


def _rwkv_scan_kernel(r_ref, lw_ref, k_ref, v_ref, a_ref, b_ref, y_ref, state_sc):
    c = SCAN_CHUNK
    n = HEADS_PER_TILE * c
    n_pairs = r_ref.shape[2] // LANES
    items = [(p, cj) for cj in range(r_ref.shape[1] // c) for p in range(n_pairs)]
    chunks = range(len(items))

    @pl.when(pl.program_id(2) == 0)
    def _():
        state_sc[...] = jnp.zeros(state_sc.shape, F32)

    head0 = lax.broadcasted_iota(jnp.int32, (c, LANES), 1) < HEAD_DIM

    def stack(z):
        return jnp.concatenate([jnp.where(head0, z, 0.0), jnp.where(head0, 0.0, z)], axis=0)

    def stack_bf16(z):
        return stack(z).astype(BF16)

    tril_ones = jnp.where(lax.broadcasted_iota(jnp.int32, (c, c), 0)
                          >= lax.broadcasted_iota(jnp.int32, (c, c), 1), 1.0, 0.0).astype(BF16)
    t_row = lax.broadcasted_iota(jnp.int32, (c, LANES), 0)
    s_col = jnp.bitwise_and(lax.broadcasted_iota(jnp.int32, (c, LANES), 1), HEAD_DIM - 1)
    strict = t_row > s_col
    incl = t_row >= s_col

    rt, at, bk, vs, w_end = [], [], [], [], []
    for p, cj in items:
        rows = slice(cj * c, (cj + 1) * c)
        lanes = slice(p * LANES, (p + 1) * LANES)
        lw = lw_ref[0, rows, lanes]
        lw_hi = lw.astype(BF16)
        lw_mid = (lw - lw_hi.astype(F32)).astype(BF16)
        lw_lo = (lw - lw_hi.astype(F32) - lw_mid.astype(F32)).astype(BF16)
        cum = (jnp.dot(tril_ones, lw_hi, preferred_element_type=F32)
               + jnp.dot(tril_ones, lw_mid, preferred_element_type=F32)
               + jnp.dot(tril_ones, lw_lo, preferred_element_type=F32))
        w_incl = jnp.exp(cum)
        w_inv = jnp.exp(-cum)
        rt.append(r_ref[0, rows, lanes] * w_incl)
        at.append(a_ref[0, rows, lanes] * jnp.exp(cum - lw))
        bk.append(jnp.concatenate([stack(b_ref[0, rows, lanes] * w_inv), stack(k_ref[0, rows, lanes] * w_inv)],
                                  axis=0).astype(BF16))
        vs.append(stack(v_ref[0, rows, lanes]))
        w_end.append(w_incl[c - 1:c, :])

    m = [_bdot_nt(jnp.concatenate([at[ci], rt[ci]], axis=0), bk[ci]) for ci in chunks]
    a_pow = [jnp.where(strict, m[ci][:c, :n], 0.0) for ci in chunks]
    inv_m1 = a_pow
    reach = 2
    while reach < c:
        pow_s = [stack_bf16(a_pow[ci]) for ci in chunks]
        a_pow = [_bdot(a_pow[ci], pow_s[ci]) for ci in chunks]
        pow_s = [stack_bf16(a_pow[ci]) for ci in chunks]
        inv_m1 = [inv_m1[ci] + a_pow[ci] + _bdot(inv_m1[ci], pow_s[ci]) for ci in chunks]
        reach *= 2
    akv = [_bdot(jnp.where(strict, m[ci][:c, n:], 0.0), vs[ci]) for ci in chunks]
    x = [jnp.concatenate([at[ci], akv[ci]], axis=1)
         + _bdot(inv_m1[ci], jnp.concatenate([stack(at[ci]), stack(akv[ci])], axis=1))
         for ci in chunks]
    xs = [jnp.concatenate([stack(x[ci][:, :LANES]), stack(x[ci][:, LANES:])], axis=1)
          for ci in chunks]
    z = [_bdot(jnp.where(incl, m[ci][c:, :n], 0.0), xs[ci]) for ci in chunks]
    r_eff = [rt[ci] + z[ci][:, :LANES] for ci in chunks]
    y_loc = [z[ci][:, LANES:] + _bdot(jnp.where(incl, m[ci][c:, n:], 0.0), vs[ci]) for ci in chunks]
    g = [_bdot(xs[ci][:, :LANES].T, bk[ci][:n]) for ci in chunks]
    h = [_bdot(jnp.concatenate([xs[ci][:, LANES:], vs[ci]], axis=0).T, bk[ci]) for ci in chunks]

    states = [state_sc[p] for p in range(n_pairs)]
    for ci, (p, cj) in enumerate(items):
        y_ref[0, cj * c:(cj + 1) * c, p * LANES:(p + 1) * LANES] = _bdot_nt(r_eff[ci], states[p]) + y_loc[ci]
        states[p] = (states[p] + _bdot(states[p], g[ci]) + h[ci]) * w_end[ci]
    for p in range(n_pairs):
        state_sc[p] = states[p]


def _rwkv_scan(r, lw, k, v, a, b):
    bsz, t, _ = r.shape
    tc = min(SCAN_CHUNK * SCAN_CHUNKS_PER_STEP, t)
    spec = pl.BlockSpec((1, tc, SCAN_PAIRS_PER_STEP * LANES), lambda bi, p, c: (bi, c, p))
    return pl.pallas_call(
        _rwkv_scan_kernel,
        grid=(bsz, N_PAIRS // SCAN_PAIRS_PER_STEP, t // tc),
        in_specs=[spec] * 6,
        out_specs=spec,
        out_shape=jax.ShapeDtypeStruct((bsz, t, MIX_WIDTH), F32),
        scratch_shapes=[pltpu.VMEM((SCAN_PAIRS_PER_STEP, LANES, LANES), F32)],
        compiler_params=_params("parallel", "parallel", "arbitrary"),
        name="rwkv_scan",
    )(r, lw, k, v, a, b)


def _tail_kernel(*refs, rwkv, alpha):
    if rwkv:
        y_ref, bonus_ref, gate_ref, gng_ref, gnb_ref, seg_ref = refs[:6]
        refs = refs[6:]
    else:
        y_ref = refs[0]
        refs = refs[1:]
    (qm_ref, mk_ref, mv_ref, x_ref, wo_mix_ref, wo_mem_ref, ln1g_ref, ln1b_ref,
     w1_ref, w2_ref, ln2g_ref, ln2b_ref, o_ref) = refs

    m_tokens = mk_ref.shape[1] // MEM_HEADS
    tm = x_ref.shape[1]
    groups = [slice(i * (tm // ROW_GROUPS), (i + 1) * (tm // ROW_GROUPS)) for i in range(ROW_GROUPS)]

    def mixed_and_normed(rows):
        if rwkv:
            y = y_ref[0, rows, :]
            seg = seg_ref[...]
            mean = _head_sums(y, seg) * (1.0 / HEAD_DIM)
            d = y - mean
            var = _head_sums(d * d, seg) * (1.0 / HEAD_DIM)
            yn = d * lax.rsqrt(var + RWKV_GN_EPS) * gng_ref[...] + gnb_ref[...]
            o_mix = ((yn + bonus_ref[0, rows, :]) * gate_ref[0, rows, :]).astype(BF16)
        else:
            o_mix = y_ref[0, rows, :]
        s = lax.dot_general(qm_ref[0, rows, :], mk_ref[0], (((1,), (1,)), ((), ())),
                            preferred_element_type=F32)
        probs = []
        for h in range(MEM_HEADS):
            sh = s[:, h * m_tokens:(h + 1) * m_tokens]
            e = jnp.exp(sh - jnp.max(sh, axis=1, keepdims=True))
            probs.append((e / jnp.sum(e, axis=1, keepdims=True)).astype(BF16))
        o_mem = jnp.dot(jnp.concatenate(probs, axis=1), mv_ref[0], preferred_element_type=F32)
        mix = (jnp.dot(o_mix, wo_mix_ref[...], preferred_element_type=F32)
               + _bdot(o_mem, wo_mem_ref[...]))
        return _layer_norm(alpha * x_ref[0, rows, :] + mix, ln1g_ref[...], ln1b_ref[...])

    def mlp(x1):
        hidden = jnp.dot(x1.astype(BF16), w1_ref[...], preferred_element_type=F32)
        hidden = jnp.square(jnp.maximum(hidden, 0.0)).astype(BF16)
        ff = jnp.dot(hidden, w2_ref[...], preferred_element_type=F32)
        return _layer_norm(alpha * x1 + ff, ln2g_ref[...], ln2b_ref[...])

    x1 = [mixed_and_normed(rows) for rows in groups]
    for rows, x1_rows in zip(groups, x1):
        o_ref[0, rows, :] = mlp(x1_rows)


def _tail(mix_inputs, qm, mem_k, mem_v, x, wo_mix, wo_mem, ln1g, ln1b, w1, w2, ln2g, ln2b, *, rwkv, alpha):
    bsz, t, d = x.shape
    tm = min(ROW_TILE, t)
    row = lambda width: pl.BlockSpec((1, tm, width), lambda b, i: (b, i, 0))
    mem_spec = pl.BlockSpec((1,) + mem_k.shape[1:], lambda b, i: (b, 0, 0))
    single = lambda a: pl.BlockSpec(a.shape, lambda b, i: (0,) * a.ndim, pipeline_mode=pl.Buffered(1))
    if rwkv:
        y, bonus, gate, gng, gnb, seg = mix_inputs
        head_args = [y, bonus, gate, gng, gnb, seg]
        head_specs = [row(MIX_WIDTH)] * 3 + [single(gng), single(gnb), single(seg)]
    else:
        head_args = list(mix_inputs)
        head_specs = [row(MIX_WIDTH)]
    consts = (wo_mix, wo_mem, ln1g, ln1b, w1, w2, ln2g, ln2b)
    return pl.pallas_call(
        functools.partial(_tail_kernel, rwkv=rwkv, alpha=alpha),
        grid=(bsz, t // tm),
        in_specs=head_specs + [row(MEM_WIDTH), mem_spec, mem_spec, row(d)] + [single(c) for c in consts],
        out_specs=row(d),
        out_shape=jax.ShapeDtypeStruct((bsz, t, d), F32),
        compiler_params=_params("parallel", "parallel"),
        name="rwkv_tail" if rwkv else "mla_tail",
    )(*head_args, qm, mem_k, mem_v, x, *consts)


def _swap_halves(w):
    half = w.shape[-1] // 2
    return jnp.concatenate([w[..., half:], w[..., :half]], axis=-1)


def _pair_pack(w):
    kdim = w.shape[0]
    w = w.reshape(kdim, N_PAIRS, HEADS_PER_TILE * MLA_ROPE_DIM)
    pad = jnp.zeros((kdim, N_PAIRS, LANES - HEADS_PER_TILE * MLA_ROPE_DIM), w.dtype)
    return jnp.concatenate([w, pad], axis=-1).reshape(kdim, N_PAIRS * LANES)


def _mla_weights(w_in, w_q_up, w_kv_up):
    d = w_in.shape[0]
    o_kv, o_kr, o_qm = MLA_Q_RANK, MLA_Q_RANK + MLA_KV_RANK, MLA_Q_RANK + MLA_KV_RANK + MLA_ROPE_DIM
    w_kr = w_in[:, o_kr:o_qm]
    pad = jnp.zeros((d, LANES - 2 * MLA_ROPE_DIM), w_in.dtype)
    w_in_p = jnp.concatenate([w_in[:, :o_kr], w_in[:, o_qm:], w_kr, w_kr, pad,
                              _swap_halves(w_kr), _swap_halves(w_kr), pad], axis=1)
    wq = w_q_up.reshape(MLA_Q_RANK, MIX_HEADS, MLA_QK_DIM)
    rope = wq[:, :, MLA_NOPE_DIM:]
    w_q = jnp.concatenate([wq[:, :, :MLA_NOPE_DIM].reshape(MLA_Q_RANK, MIX_WIDTH),
                           _pair_pack(rope), _pair_pack(_swap_halves(rope))], axis=1)
    wkv = w_kv_up.reshape(MLA_KV_RANK, MIX_HEADS, 2 * HEAD_DIM)
    w_kv = jnp.concatenate([wkv[:, :, :HEAD_DIM].reshape(MLA_KV_RANK, MIX_WIDTH),
                            wkv[:, :, HEAD_DIM:].reshape(MLA_KV_RANK, MIX_WIDTH)], axis=1)
    return w_in_p.astype(BF16), w_q.astype(BF16), w_kv.astype(BF16)


def _rope_tables(positions):
    lane = jnp.arange(LANES)
    inv_freq = ROPE_BASE ** (-(lane % ROPE_HALF).astype(F32) * 2.0 / MLA_ROPE_DIM)
    ang = positions.astype(F32).reshape(-1, 1) * inv_freq
    used = lane < HEADS_PER_TILE * MLA_ROPE_DIM
    sign = jnp.where((lane // ROPE_HALF) % 2 == 0, -1.0, 1.0).astype(F32)
    return jnp.where(used, jnp.cos(ang), 0.0), jnp.where(used, sign * jnp.sin(ang), 0.0)


def _pad_cols(w, width):
    return jnp.concatenate([w, jnp.zeros(w.shape[:-1] + (width - w.shape[-1],), w.dtype)], axis=-1)


def _pad_rows(w, height):
    return jnp.concatenate([w, jnp.zeros((height - w.shape[0],) + w.shape[1:], w.dtype)], axis=0)


def _rwkv_weights(w_in, mu, w2, a2, g2):
    o_hw, o_ha = 3 * MIX_WIDTH, 3 * MIX_WIDTH + RWKV_DECAY_RANK
    o_hg = o_ha + RWKV_A_RANK
    o_qm = o_hg + RWKV_GATE_RANK

    def layout(w):
        return jnp.concatenate([w[..., :o_hw], _pad_cols(w[..., o_hw:o_ha], LANES),
                                _pad_cols(w[..., o_ha:o_hg], LANES),
                                _pad_cols(w[..., o_hg:o_qm], 2 * LANES)], axis=-1)

    w_in_p = jnp.concatenate([layout(w_in[:, :o_qm]), w_in[:, o_qm:]], axis=1)
    return (w_in_p.astype(BF16), layout(mu).reshape(1, RW_TM_COLS),
            _pad_rows(w2, LANES).astype(BF16), _pad_rows(a2, LANES).astype(BF16),
            _pad_rows(g2, 2 * LANES).astype(BF16))


def kernel(x, mem, positions, mem_ln_g, mem_ln_b, w_mem_kv, mla_w_in, mla_q_norm, mla_w_q_up, mla_kv_norm, mla_w_kv_up, rwkv_w_in, rwkv_mu, rwkv_w0, rwkv_w2, rwkv_a0, rwkv_a2, rwkv_g2, rwkv_k_k, rwkv_k_a, rwkv_r_k, rwkv_gn_g, rwkv_gn_b, w_out, ln1_g, ln1_b, w_ff1, w_ff2, ln2_g, ln2_b):
    bsz, t, d = x.shape
    depth = w_out.shape[0]
    alpha = (2.0 * depth) ** 0.25
    row = lambda p: p.reshape(1, -1)

    mem_k, mem_v = _mem_kv(mem, mem_ln_g, mem_ln_b, w_mem_kv.astype(BF16))
    cos, sin = _rope_tables(positions)
    head_of = jnp.arange(MXU_DIM) // HEAD_DIM
    seg = (head_of[:, None] == head_of[None, :]).astype(BF16)

    for i in range(depth):
        j = i // 2
        tail_w = (w_out[i, :MIX_WIDTH].astype(BF16), w_out[i, MIX_WIDTH:].astype(BF16),
                  row(ln1_g[i]), row(ln1_b[i]), w_ff1[i].astype(BF16), w_ff2[i].astype(BF16),
                  row(ln2_g[i]), row(ln2_b[i]))
        if i % 2 == 0:
            w_in, w_q, w_kv = _mla_weights(mla_w_in[j], mla_w_q_up[j], mla_w_kv_up[j])
            qn, qr, kn, kr, v, qm = _mla_proj(x.reshape(bsz * t, d), cos, sin, w_in, row(mla_q_norm[j]), w_q,
                                              row(mla_kv_norm[j]), w_kv)
            seq = lambda a: a.reshape(bsz, t, a.shape[-1])
            o_mix = _mla_attn(seq(qn), seq(qr), seq(kn), seq(kr), seq(v))
            x = _tail((o_mix,), seq(qm), mem_k, mem_v, x, *tail_w, rwkv=False, alpha=alpha)
        else:
            w_in, mu, w2, a2, g2 = _rwkv_weights(rwkv_w_in[j], rwkv_mu[j], rwkv_w2[j], rwkv_a2[j], rwkv_g2[j])
            r, lw, k, v, a, b, gate, bonus, qm = _rwkv_proj(
                x, w_in, mu, row(rwkv_w0[j]), w2, row(rwkv_a0[j]), a2, g2, row(rwkv_k_k[j]), row(rwkv_k_a[j]),
                row(rwkv_r_k[j]), seg)
            y = _rwkv_scan(r, lw, k, v, a, b)
            x = _tail((y, bonus, gate, row(rwkv_gn_g[j]), row(rwkv_gn_b[j]), seg), qm, mem_k, mem_v, x,
                      *tail_w, rwkv=True, alpha=alpha)
    return x
```
